```python
import jax, jax.numpy as jnp
from jax import lax
import numpy as np

D_MODEL = 1024
BATCH = 2
SEQ = 8192
DEPTH = 4
DEC_BATCH = 32
DEC_SEQ = 1
PAST_LEN = 8192
PAGE_SIZE = 128

N_MIXERS = 2
N_ATTN_LAYERS = (DEPTH + 1) // 2
N_CONV_LAYERS = DEPTH // 2
N_HEADS = 16
HEAD_DIM = D_MODEL // N_HEADS
N_KV_HEADS = 4
GROUP = N_HEADS // N_KV_HEADS
IDX_HEADS = 8
IDX_DIM = 64
TOPK_MAX = 256
Q_BLOCK = 128
CONV_WIDTH = 3
D_FF = 2816
N_MOD = 9
RMS_EPS = 1e-6
ATTN_SPLITS = (N_HEADS * HEAD_DIM, N_KV_HEADS * HEAD_DIM, N_KV_HEADS * HEAD_DIM,
               IDX_HEADS * IDX_DIM, IDX_DIM, IDX_HEADS)
ATTN_IN_COLS = sum(ATTN_SPLITS)

kernel_name = 'hybrid_dsa_shortconv_macaron_adaln_step'


def rms_norm(x):
    x32 = x.astype(jnp.float32)
    y = x32 * lax.rsqrt(jnp.mean(x32 * x32, axis=-1, keepdims=True) + RMS_EPS)
    return y.astype(x.dtype)


def adaln_params(c, w_ada, b_ada):
    m = jax.nn.silu(c) @ w_ada + b_ada
    return m.reshape(c.shape[0], N_MOD, D_MODEL)[:, :, None, :]


def modulate(x, shift, scale):
    return rms_norm(x) * (1 + scale) + shift


def swiglu(h, w_in, w_out):
    a, g = jnp.split(h @ w_in, 2, axis=-1)
    return (jax.nn.silu(a) * g) @ w_out


def split_cols(p, sizes):
    offsets = np.cumsum(np.array(sizes))[:-1].tolist()
    return jnp.split(p, offsets, axis=-1)


def gather_rows(rows, idx):
    return jax.vmap(lambda r, i: r[i])(rows, idx)


def attn_project(h, w_in, q_gain, k_gain):
    B, T, _ = h.shape
    q, k, v, qi, ki, wi = split_cols(h @ w_in, ATTN_SPLITS)
    q = rms_norm(q.reshape(B, T, N_HEADS, HEAD_DIM)) * q_gain
    k = rms_norm(k.reshape(B, T, N_KV_HEADS, HEAD_DIM)) * k_gain
    v = v.reshape(B, T, N_KV_HEADS, HEAD_DIM)
    qi = qi.reshape(B, T, IDX_HEADS, IDX_DIM)
    wi = wi * IDX_HEADS ** -0.5
    return q, k, v, qi, ki, wi


def indexer_scores(qi, ki, wi):
    s = jnp.einsum('bqhd,bsd->bqhs', qi, ki).astype(jnp.float32) * IDX_DIM ** -0.5
    return jnp.einsum('bqh,bqhs->bqs', wi.astype(jnp.float32), jax.nn.relu(s))


def select_topk(scores, qpos, topk):
    keypos = jnp.arange(scores.shape[-1])
    scores = jnp.where(keypos[None, None, :] <= qpos[None, :, None], scores, -jnp.inf)
    _, idx = lax.top_k(scores, topk)
    return idx, idx <= qpos[None, :, None]


def attend_selected(q, ksel, vsel, valid):
    B, T = q.shape[:2]
    qg = q.reshape(B, T, N_KV_HEADS, GROUP, HEAD_DIM)
    logits = jnp.einsum('bqhgd,bqnhd->bqhgn', qg, ksel).astype(jnp.float32) * HEAD_DIM ** -0.5
    logits = jnp.where(valid[:, :, None, None, :], logits, -jnp.inf)
    p = jax.nn.softmax(logits, axis=-1).astype(vsel.dtype)
    o = jnp.einsum('bqhgn,bqnhd->bqhgd', p, vsel)
    return o.reshape(B, T, N_HEADS * HEAD_DIM)


def dsa_prompt(q, k, v, qi, ki, wi):
    B, T = q.shape[:2]
    topk = min(TOPK_MAX, T // 4)
    nb = T // Q_BLOCK

    def blocks(a):
        return jnp.moveaxis(a.reshape(B, nb, Q_BLOCK, *a.shape[2:]), 1, 0)

    def one_block(args):
        i, qb, qib, wib = args
        qpos = i * Q_BLOCK + jnp.arange(Q_BLOCK)
        idx, valid = select_topk(indexer_scores(qib, ki, wib), qpos, topk)
        return attend_selected(qb, gather_rows(k, idx), gather_rows(v, idx), valid)

    out = lax.map(one_block, (jnp.arange(nb), blocks(q), blocks(qi), blocks(wi)))
    return jnp.moveaxis(out, 0, 1).reshape(B, T, N_HEADS * HEAD_DIM)


def dsa_sample(q, k, v, qi, ki, wi, cache_k, cache_v, cache_kidx, page_table):
    DB, TN = q.shape[:2]
    n_pages = page_table.shape[1]
    past = n_pages * PAGE_SIZE
    topk = min(TOPK_MAX, (past + TN) // 4)
    ki_past = cache_kidx[page_table].reshape(DB, past, IDX_DIM)
    ki_all = jnp.concatenate([ki_past, ki.astype(ki_past.dtype)], axis=1)
    qpos = past + jnp.arange(TN)
    idx, valid = select_topk(indexer_scores(qi, ki_all, wi), qpos, topk)
    in_past = (idx < past)[..., None, None]
    page = gather_rows(page_table, jnp.minimum(idx // PAGE_SIZE, n_pages - 1))
    phys = page * PAGE_SIZE + idx % PAGE_SIZE
    k_pool = cache_k.reshape(-1, N_KV_HEADS, HEAD_DIM)
    v_pool = cache_v.reshape(-1, N_KV_HEADS, HEAD_DIM)
    new_i = jnp.clip(idx - past, 0, TN - 1)
    ksel = jnp.where(in_past, k_pool[phys], gather_rows(k, new_i))
    vsel = jnp.where(in_past, v_pool[phys], gather_rows(v, new_i))
    return attend_selected(q, ksel, vsel, valid)


def short_conv(h, prefix, w_in, conv_w, w_out):
    T = h.shape[1]
    b_gate, c_gate, u = jnp.split(h @ w_in, 3, axis=-1)
    z = jnp.concatenate([prefix.astype(h.dtype), c_gate * u], axis=1)
    y = conv_w[0] * z[:, 0:T]
    for j in range(1, CONV_WIDTH):
        y = y + conv_w[j] * z[:, j:j + T]
    return (b_gate * y) @ w_out, z[:, z.shape[1] - (CONV_WIDTH - 1):]


def macaron_layer(x, mod, ffn1, ffn2, mixer):
    x = x + 0.5 * mod[:, 2] * swiglu(modulate(x, mod[:, 0], mod[:, 1]), ffn1[0], ffn1[1])
    mix_out, state = mixer(modulate(x, mod[:, 3], mod[:, 4]))
    x = x + mod[:, 5] * mix_out
    x = x + 0.5 * mod[:, 8] * swiglu(modulate(x, mod[:, 6], mod[:, 7]), ffn2[0], ffn2[1])
    return x, state


def setup_inputs(seed: int = 0) -> dict:
    key = jax.random.key(seed)
    ks = jax.random.split(key, 20)
    n_pages = PAST_LEN // PAGE_SIZE
    n_used = DEC_BATCH * n_pages
    n_pool = n_used + max(1, n_used // 4)

    def nrm(k, shape, scale=1.0):
        return jax.random.normal(k, shape, jnp.float32) * scale

    page_table = jax.random.permutation(ks[0], n_pool)[:n_used].reshape(DEC_BATCH, n_pages).astype(jnp.int32)
    return {
        'x_prompt': nrm(ks[1], (BATCH, SEQ, D_MODEL)),
        'x_sample': nrm(ks[2], (DEC_BATCH, DEC_SEQ, D_MODEL)),
        'cache_k': nrm(ks[3], (N_ATTN_LAYERS, n_pool, PAGE_SIZE, N_KV_HEADS, HEAD_DIM)),
        'cache_v': nrm(ks[4], (N_ATTN_LAYERS, n_pool, PAGE_SIZE, N_KV_HEADS, HEAD_DIM)),
        'cache_kidx': nrm(ks[5], (N_ATTN_LAYERS, n_pool, PAGE_SIZE, IDX_DIM)),
        'state_conv': nrm(ks[6], (N_CONV_LAYERS, DEC_BATCH, CONV_WIDTH - 1, D_MODEL)),
        'page_table': page_table,
        'c_prompt': nrm(ks[7], (BATCH, D_MODEL)),
        'c_sample': nrm(ks[8], (DEC_BATCH, D_MODEL)),
        'w_ada': nrm(ks[9], (DEPTH, D_MODEL, N_MOD * D_MODEL), 0.5 * D_MODEL ** -0.5),
        'b_ada': nrm(ks[10], (DEPTH, N_MOD * D_MODEL), 0.02),
        'w_ffn_in': nrm(ks[11], (DEPTH, 2, D_MODEL, 2 * D_FF), D_MODEL ** -0.5),
        'w_ffn_out': nrm(ks[12], (DEPTH, 2, D_FF, D_MODEL), D_FF ** -0.5),
        'w_attn_in': nrm(ks[13], (N_ATTN_LAYERS, D_MODEL, ATTN_IN_COLS), D_MODEL ** -0.5),
        'w_attn_out': nrm(ks[14], (N_ATTN_LAYERS, N_HEADS * HEAD_DIM, D_MODEL), (N_HEADS * HEAD_DIM) ** -0.5),
        'q_norm_gain': 1.0 + nrm(ks[15], (N_ATTN_LAYERS, HEAD_DIM), 0.05),
        'k_norm_gain': 1.0 + nrm(ks[16], (N_ATTN_LAYERS, HEAD_DIM), 0.05),
        'w_conv_in': nrm(ks[17], (N_CONV_LAYERS, D_MODEL, 3 * D_MODEL), D_MODEL ** -0.5),
        'conv_w': nrm(ks[18], (N_CONV_LAYERS, CONV_WIDTH, D_MODEL), CONV_WIDTH ** -0.5),
        'w_conv_out': nrm(ks[19], (N_CONV_LAYERS, D_MODEL, D_MODEL), D_MODEL ** -0.5),
    }


def reference(x_prompt, x_sample, cache_k, cache_v, cache_kidx, state_conv, page_table,
              c_prompt, c_sample, w_ada, b_ada, w_ffn_in, w_ffn_out, w_attn_in, w_attn_out,
              q_norm_gain, k_norm_gain, w_conv_in, conv_w, w_conv_out):
    xp, xs = x_prompt, x_sample
    pk, pv, pki, pconv = [], [], [], []
    sk, sv, ski, sconv = [], [], [], []
    for layer in range(DEPTH):
        mp = adaln_params(c_prompt, w_ada[layer], b_ada[layer])
        ms = adaln_params(c_sample, w_ada[layer], b_ada[layer])
        ffn1 = (w_ffn_in[layer, 0], w_ffn_out[layer, 0])
        ffn2 = (w_ffn_in[layer, 1], w_ffn_out[layer, 1])
        li = layer // N_MIXERS
        if layer % N_MIXERS == 0:
            wa_in, wa_out = w_attn_in[li], w_attn_out[li]
            qg, kg = q_norm_gain[li], k_norm_gain[li]

            def prompt_mixer(h):
                q, k, v, qi, ki, wi = attn_project(h, wa_in, qg, kg)
                return dsa_prompt(q, k, v, qi, ki, wi) @ wa_out, (k, v, ki)

            def sample_mixer(h):
                q, k, v, qi, ki, wi = attn_project(h, wa_in, qg, kg)
                o = dsa_sample(q, k, v, qi, ki, wi, cache_k[li], cache_v[li], cache_kidx[li], page_table)
                return o @ wa_out, (k, v, ki)

            xp, (k_p, v_p, ki_p) = macaron_layer(xp, mp, ffn1, ffn2, prompt_mixer)
            xs, (k_s, v_s, ki_s) = macaron_layer(xs, ms, ffn1, ffn2, sample_mixer)
            pk.append(k_p); pv.append(v_p); pki.append(ki_p)
            sk.append(k_s); sv.append(v_s); ski.append(ki_s)
        else:
            wc_in, cw, wc_out = w_conv_in[li], conv_w[li], w_conv_out[li]
            zero_prefix = jnp.zeros((xp.shape[0], CONV_WIDTH - 1, D_MODEL), xp.dtype)
            past_prefix = state_conv[li]
            xp, st_p = macaron_layer(xp, mp, ffn1, ffn2,
                                     lambda h: short_conv(h, zero_prefix, wc_in, cw, wc_out))
            xs, st_s = macaron_layer(xs, ms, ffn1, ffn2,
                                     lambda h: short_conv(h, past_prefix, wc_in, cw, wc_out))
            pconv.append(st_p)
            sconv.append(st_s)
    new_k_prompt = jnp.stack(pk)
    new_v_prompt = jnp.stack(pv)
    new_kidx_prompt = jnp.stack(pki)
    new_conv_prompt = jnp.stack(pconv)
    new_k_sample = jnp.stack(sk)
    new_v_sample = jnp.stack(sv)
    new_kidx_sample = jnp.stack(ski)
    new_conv_sample = jnp.stack(sconv)
    return (xp, xs, new_k_prompt, new_v_prompt, new_kidx_prompt, new_conv_prompt,
            new_k_sample, new_v_sample, new_kidx_sample, new_conv_sample)
```

```python
import functools

import jax
import jax.numpy as jnp
from jax import lax
from jax.experimental import pallas as pl
from jax.experimental.pallas import tpu as pltpu

F32 = jnp.float32
BF16 = jnp.bfloat16
I32 = jnp.int32

RMS_EPS = 1e-6
TOPK_MAX = 256
N_MOD = 9
SUBLANES = 8
NEG_INF = float("-inf")
INT_MIN = -(2 ** 31)
KEY_NEG_INF = (0xFF800000 ^ 0x7FFFFFFF) - (1 << 32)

VMEM_LIMIT_BYTES = 56 * 1024 * 1024
ROW_TILE = 256
FFN_ROW_TILE = 512
FF_CHUNK = 256


def _params(*sem):
    return pltpu.CompilerParams(dimension_semantics=sem, vmem_limit_bytes=VMEM_LIMIT_BYTES)


def _sigmoid(x):
    return 1.0 / (1.0 + jnp.exp(-x))


def _modulate(x, shift, scale):
    ms = jnp.mean(x * x, axis=-1, keepdims=True)
    return (x * lax.rsqrt(ms + RMS_EPS)) * (1.0 + scale) + shift


def _dot(a, b):
    return jnp.dot(a, b, preferred_element_type=F32)


def _dot_nt(a, b):
    return lax.dot_general(a, b, (((1,), (1,)), ((), ())), preferred_element_type=F32)


def _float_key(x):
    bits = lax.bitcast_convert_type(x, I32)
    return bits ^ (lax.shift_right_arithmetic(bits, 31) & 0x7FFFFFFF)


def _fold_rows(x, op):
    acc = x[0:SUBLANES]
    for j in range(1, x.shape[0] // SUBLANES):
        acc = op(acc, x[j * SUBLANES:(j + 1) * SUBLANES])
    return acc


def _ada_kernel(c_ref, w_ref, b_ref, o_ref):
    c = c_ref[...]
    s = (c * _sigmoid(c)).astype(BF16)
    o_ref[0] = _dot(s, w_ref[0].astype(BF16)) + b_ref[0]


def ada_mods(c_all, w_ada, b_ada):
    depth, d, nd = w_ada.shape
    r = c_all.shape[0]
    return pl.pallas_call(
        _ada_kernel,
        grid=(depth, nd // d),
        in_specs=[
            pl.BlockSpec((r, d), lambda l, j: (0, 0)),
            pl.BlockSpec((1, d, d), lambda l, j: (l, 0, j)),
            pl.BlockSpec((1, 1, d), lambda l, j: (l, 0, j)),
        ],
        out_specs=pl.BlockSpec((1, r, d), lambda l, j: (l, 0, j)),
        out_shape=jax.ShapeDtypeStruct((depth, r, nd), F32),
        compiler_params=_params("arbitrary", "arbitrary"),
        name="ada_mods",
    )(c_all, w_ada, b_ada.reshape(depth, 1, nd))


def _ffn_kernel(*refs, has_pre, d_ff, tf):
    if has_pre:
        (x_ref, a_ref, wpre_ref, gpre_ref, sh_ref, sc_ref, g_ref, win_ref, wout_ref,
         o_ref, acc_ref) = refs
    else:
        x_ref, sh_ref, sc_ref, g_ref, win_ref, wout_ref, o_ref, acc_ref = refs
    x = x_ref[...]
    if has_pre:
        x = x + gpre_ref[0] * _dot(a_ref[...], wpre_ref[...])
    hb = _modulate(x, sh_ref[0], sc_ref[0]).astype(BF16)
    for c in range(d_ff // tf):
        u = _dot(hb, win_ref[:, c * tf:(c + 1) * tf])
        g = _dot(hb, win_ref[:, d_ff + c * tf:d_ff + (c + 1) * tf])
        act = ((u * _sigmoid(u)) * g).astype(BF16)
        y = _dot(act, wout_ref[c * tf:(c + 1) * tf, :])
        if c == 0:
            acc_ref[...] = y
        else:
            acc_ref[...] += y
    o_ref[...] = x + 0.5 * g_ref[0] * acc_ref[...]


def ffn_call(x, shift, scale, gate, w_in, w_out, pre=None, tm=FFN_ROW_TILE):
    n, d = x.shape
    tm = min(tm, n)
    groups = shift.shape[0]
    tiles_per_group = n // tm // groups
    d_ff = w_out.shape[0]
    tf = FF_CHUNK if d_ff % FF_CHUNK == 0 else d_ff

    def row_spec(width):
        return pl.BlockSpec((tm, width), lambda i: (i, 0))

    def mod_spec(m):
        return pl.BlockSpec((1,) + m.shape[1:], lambda i: (i // tiles_per_group, 0, 0))

    def whole(w):
        return pl.BlockSpec(w.shape, lambda i: (0, 0), pipeline_mode=pl.Buffered(1))

    args, specs = [x], [row_spec(d)]
    if pre is not None:
        a, w_pre, g_pre = pre
        args += [a, w_pre, g_pre]
        specs += [row_spec(a.shape[1]), whole(w_pre), mod_spec(g_pre)]
    args += [shift, scale, gate, w_in, w_out]
    specs += [mod_spec(shift), mod_spec(scale), mod_spec(gate), whole(w_in), whole(w_out)]
    return pl.pallas_call(
        functools.partial(_ffn_kernel, has_pre=pre is not None, d_ff=d_ff, tf=tf),
        grid=(n // tm,),
        in_specs=specs,
        out_specs=row_spec(d),
        out_shape=jax.ShapeDtypeStruct((n, d), F32),
        scratch_shapes=[pltpu.VMEM((tm, d), F32)],
        compiler_params=_params("arbitrary"),
        name="ffn_pre" if pre is not None else "ffn",
    )(*args)


def _head_mean_square(y, g_ref):
    y2 = (y * y).astype(BF16)
    w = g_ref.shape[0]
    parts = [_dot(y2[:, j * w:(j + 1) * w], g_ref[...]) for j in range(y.shape[1] // w)]
    return parts[0] if len(parts) == 1 else jnp.concatenate(parts, axis=1)


def _project(x, sh, sc, wq_ref, wk_ref, wv_ref, wqi_ref, wkw_ref, gq_ref, gk_ref, g_ref,
             qk_scale):
    hb = _modulate(x, sh, sc).astype(BF16)
    q = _dot(hb, wq_ref[...])
    qn = q * lax.rsqrt(_head_mean_square(q, g_ref) + RMS_EPS) * gq_ref[...] * qk_scale
    k = _dot(hb, wk_ref[...])
    kn = k * lax.rsqrt(_head_mean_square(k, g_ref) + RMS_EPS) * gk_ref[...]
    v = _dot(hb, wv_ref[...])
    qi = _dot(hb, wqi_ref[...])
    kw = _dot(hb, wkw_ref[...])
    return qn, kn, v, qi, kw


def _proj_kernel(x_ref, sh_ref, sc_ref, wq_ref, wk_ref, wv_ref, wqi_ref, wkw_ref, gq_ref,
                 gk_ref, g_ref,
                 qt_ref, k_ref, kb_ref, v_ref, vt_ref, qit_ref, ki_ref, kib_ref, wit_ref,
                 *, idx_dim, idx_heads, qk_scale, wi_scale):
    qn, kn, v, qi, kw = _project(x_ref[0], sh_ref[0], sc_ref[0], wq_ref, wk_ref, wv_ref,
                                 wqi_ref, wkw_ref, gq_ref, gk_ref, g_ref, qk_scale)
    qt_ref[0, 0] = qn.T.astype(BF16)
    k_ref[0] = kn
    kb_ref[0] = kn.astype(BF16)
    v_ref[0] = v
    vt_ref[0, 0] = v.T.astype(BF16)
    qit_ref[0, 0] = qi.T.astype(BF16)
    ki = kw[:, :idx_dim]
    ki_ref[0] = ki
    kib_ref[0] = ki.astype(BF16)
    wit_ref[0, 0] = kw.T[idx_dim:idx_dim + idx_heads, :] * wi_scale


def _sample_proj_kernel(x_ref, sh_ref, sc_ref, wq_ref, wk_ref, wv_ref, wqi_ref, wkw_ref,
                        gq_ref, gk_ref, g_ref,
                        q_ref, k_ref, v_ref, qi_ref, ki_ref, wi_ref,
                        *, idx_dim, idx_heads, qk_scale, wi_scale):
    qn, kn, v, qi, kw = _project(x_ref[...], sh_ref[0], sc_ref[0], wq_ref, wk_ref, wv_ref,
                                 wqi_ref, wkw_ref, gq_ref, gk_ref, g_ref, qk_scale)
    q_ref[...] = qn.astype(BF16)
    k_ref[...] = kn
    v_ref[...] = v
    qi_ref[...] = qi.astype(BF16)
    ki_ref[...] = kw[:, :idx_dim]
    wi_ref[...] = kw[:, idx_dim:idx_dim + idx_heads] * wi_scale


def _proj_weight_specs(ws, n_grid):
    zeros = (lambda *idx: (0, 0))
    return [pl.BlockSpec(w.shape, zeros, pipeline_mode=pl.Buffered(1)) for w in ws]


def attn_proj_prompt(x3, shift, scale, aw, dims, tm=ROW_TILE):
    b, t, d = x3.shape
    nt = t // tm
    hd, kvd, ihd, di, ih = dims["hd"], dims["kvd"], dims["ihd"], dims["idx_dim"], dims["idx_heads"]
    ws = [aw["wq"], aw["wk"], aw["wv"], aw["wqi"], aw["wkw"], aw["gq"], aw["gk"], aw["g"]]
    mod_spec = pl.BlockSpec((1, 1, d), lambda bi, i: (bi, 0, 0))
    nat = lambda w: pl.BlockSpec((1, tm, w), lambda bi, i: (bi, i, 0))
    tr = lambda r: pl.BlockSpec((1, 1, r, tm), lambda bi, i: (bi, i, 0, 0))
    return pl.pallas_call(
        functools.partial(_proj_kernel, idx_dim=di, idx_heads=ih, qk_scale=dims["qk_scale"],
                          wi_scale=dims["wi_scale"]),
        grid=(b, nt),
        in_specs=[nat(d), mod_spec, mod_spec] + _proj_weight_specs(ws, 2),
        out_specs=[tr(hd), nat(kvd), nat(kvd), nat(kvd), tr(kvd), tr(ihd), nat(di), nat(di), tr(ih)],
        out_shape=[
            jax.ShapeDtypeStruct((b, nt, hd, tm), BF16),
            jax.ShapeDtypeStruct((b, t, kvd), F32),
            jax.ShapeDtypeStruct((b, t, kvd), BF16),
            jax.ShapeDtypeStruct((b, t, kvd), F32),
            jax.ShapeDtypeStruct((b, nt, kvd, tm), BF16),
            jax.ShapeDtypeStruct((b, nt, ihd, tm), BF16),
            jax.ShapeDtypeStruct((b, t, di), F32),
            jax.ShapeDtypeStruct((b, t, di), BF16),
            jax.ShapeDtypeStruct((b, nt, ih, tm), F32),
        ],
        compiler_params=_params("arbitrary", "arbitrary"),
        name="attn_proj",
    )(x3, shift, scale, *ws)


def attn_proj_sample(x, shift, scale, aw, dims):
    n, d = x.shape
    hd, kvd, ihd, di, ih = dims["hd"], dims["kvd"], dims["ihd"], dims["idx_dim"], dims["idx_heads"]
    ws = [aw["wq"], aw["wk"], aw["wv"], aw["wqi"], aw["wkw"], aw["gq"], aw["gk"], aw["g"]]
    full = lambda w: pl.BlockSpec((n, w), lambda i: (0, 0))
    mod_spec = pl.BlockSpec((1, n, d), lambda i: (0, 0, 0))
    return pl.pallas_call(
        functools.partial(_sample_proj_kernel, idx_dim=di, idx_heads=ih,
                          qk_scale=dims["qk_scale"], wi_scale=dims["wi_scale"]),
        grid=(1,),
        in_specs=[full(d), mod_spec, mod_spec] + _proj_weight_specs(ws, 1),
        out_specs=[full(hd), full(kvd), full(kvd), full(ihd), full(di), full(ih)],
        out_shape=[
            jax.ShapeDtypeStruct((n, hd), BF16),
            jax.ShapeDtypeStruct((n, kvd), F32),
            jax.ShapeDtypeStruct((n, kvd), F32),
            jax.ShapeDtypeStruct((n, ihd), BF16),
            jax.ShapeDtypeStruct((n, di), F32),
            jax.ShapeDtypeStruct((n, ih), F32),
        ],
        compiler_params=_params("arbitrary"),
        name="attn_proj_sample",
    )(x, shift, scale, *ws)


def _dsa_kernel(qt_ref, qit_ref, wit_ref, k_ref, vt_ref, ki_ref, o_ref,
                keys_ref, qpad_ref, out_ref, jlim_ref,
                *, n_heads, group, head_dim, idx_heads, idx_dim, topk, seq_bits):
    tq = qt_ref.shape[-1]
    tk = tq
    i = pl.program_id(1)
    n_chunks = i + 1
    q_idx = i * tq + lax.broadcasted_iota(I32, (tk, tq), 1)
    row_iota = lax.broadcasted_iota(I32, (tk, tq), 0)

    def chunk_rows(c):
        return pl.ds(pl.multiple_of(c * tk, tk), tk)

    def score_chunk(c, carry):
        kic = ki_ref[0, chunk_rows(c), :]
        acc = jnp.zeros((tk, tq), F32)
        for h in range(idx_heads):
            s = _dot(kic, qit_ref[0, 0, h * idx_dim:(h + 1) * idx_dim, :])
            acc = acc + jnp.maximum(s, 0.0) * wit_ref[0, 0, h:h + 1, :]
        acc = jnp.where(c * tk + row_iota <= q_idx, acc, NEG_INF)
        keys_ref[chunk_rows(c), :] = _float_key(acc)
        return carry

    lax.fori_loop(0, n_chunks, score_chunk, 0)

    def count(pred):
        def body(c, cnt):
            kidx = c * tk + row_iota
            return cnt + _fold_rows(pred(keys_ref[chunk_rows(c), :], kidx).astype(I32), jnp.add)
        cnt8 = lax.fori_loop(0, n_chunks, body, jnp.zeros((SUBLANES, tq), I32))
        return jnp.sum(cnt8, axis=0, keepdims=True)

    def count_ge(cand):
        return count(lambda kc, kidx: kc >= cand)

    zero = jnp.zeros((1, tq), I32)
    prefix = jnp.where(count_ge(zero) >= topk, zero, INT_MIN)

    def search(it, prefix):
        cand = prefix | jnp.left_shift(jnp.int32(1), 30 - it)
        return jnp.where(count_ge(cand) >= topk, cand, prefix)

    thr = lax.fori_loop(0, 31, search, prefix)
    n_ge = count_ge(thr)
    n_gt = count(lambda kc, kidx: kc > thr)
    need = topk - n_gt

    jlim_ref[...] = jnp.full((1, tq), (1 << seq_bits) - 1, I32)
    excess = jnp.logical_and(n_ge > topk, thr > KEY_NEG_INF)

    @pl.when(jnp.max(excess.astype(I32)) > 0)
    def _():
        def tie_search(it, j):
            cand = j | jnp.left_shift(jnp.int32(1), seq_bits - 1 - it)
            n_before = count(lambda kc, kidx: jnp.logical_and(kc == thr, kidx < cand))
            return jnp.where(n_before < need, cand, j)
        jlim_ref[...] = lax.fori_loop(0, seq_bits, tie_search, zero)

    jlim = jlim_ref[...]

    def bias_chunk(c, carry):
        kc = keys_ref[chunk_rows(c), :]
        kidx = c * tk + row_iota
        sel = jnp.logical_or(kc > thr, jnp.logical_and(kc == thr, kidx <= jlim))
        sel = jnp.logical_and(sel, kidx <= q_idx)
        bias = jnp.where(sel, 0.0, NEG_INF).astype(F32)
        keys_ref[chunk_rows(c), :] = lax.bitcast_convert_type(bias, I32)
        return carry

    lax.fori_loop(0, n_chunks, bias_chunk, 0)

    kvd = k_ref.shape[-1]
    for h in range(n_heads):
        g = h // group
        qpad_ref[...] = jnp.zeros((kvd, tq), BF16)
        qpad_ref[g * head_dim:(g + 1) * head_dim, :] = qt_ref[0, 0, h * head_dim:(h + 1) * head_dim, :]

        def attend(c, carry, g=g):
            m, l, acc = carry
            s = _dot(k_ref[0, chunk_rows(c), :], qpad_ref[...])
            s = s + lax.bitcast_convert_type(keys_ref[chunk_rows(c), :], F32)
            m_new = jnp.maximum(m, jnp.max(_fold_rows(s, jnp.maximum), axis=0, keepdims=True))
            m_safe = jnp.where(m_new == NEG_INF, 0.0, m_new)
            p = jnp.exp(s - m_safe)
            alpha = jnp.exp(m - m_safe)
            l = alpha * l + jnp.sum(_fold_rows(p, jnp.add), axis=0, keepdims=True)
            pv = _dot(vt_ref[0, c, g * head_dim:(g + 1) * head_dim, :], p.astype(BF16))
            return m_new, l, alpha * acc + pv

        m0 = jnp.full((1, tq), NEG_INF, F32)
        l0 = jnp.zeros((1, tq), F32)
        acc0 = jnp.zeros((head_dim, tq), F32)
        _, l, acc = lax.fori_loop(0, n_chunks, attend, (m0, l0, acc0))
        out_ref[h * head_dim:(h + 1) * head_dim, :] = acc / l

    o_ref[0] = out_ref[...].T.astype(BF16)


def dsa_prompt(qt, qit, wit, kb, vt, kib, dims):
    b, nt, hd, tq = qt.shape
    t = nt * tq
    kvd, ihd, ih = kb.shape[-1], qit.shape[2], wit.shape[2]
    topk = min(TOPK_MAX, t // 4)
    blk = lambda r: pl.BlockSpec((1, 1, r, tq), lambda bi, i: (bi, i, 0, 0))
    per_batch3 = lambda w: pl.BlockSpec((1, t, w), lambda bi, i: (bi, 0, 0))
    return pl.pallas_call(
        functools.partial(_dsa_kernel, n_heads=dims["n_heads"], group=dims["group"],
                          head_dim=dims["head_dim"], idx_heads=ih, idx_dim=dims["idx_dim"],
                          topk=topk, seq_bits=max(1, (t - 1).bit_length())),
        grid=(b, nt),
        in_specs=[blk(hd), blk(ihd), blk(ih), per_batch3(kvd),
                  pl.BlockSpec((1, nt, kvd, tq), lambda bi, i: (bi, 0, 0, 0)),
                  per_batch3(dims["idx_dim"])],
        out_specs=pl.BlockSpec((1, tq, hd), lambda bi, i: (bi, i, 0)),
        out_shape=jax.ShapeDtypeStruct((b, t, hd), BF16),
        scratch_shapes=[pltpu.VMEM((t, tq), I32), pltpu.VMEM((kvd, tq), BF16),
                        pltpu.VMEM((hd, tq), F32), pltpu.VMEM((1, tq), I32)],
        compiler_params=_params("arbitrary", "arbitrary"),
        name="dsa_prompt",
    )(qt, qit, wit, kb, vt, kib)


def _dsa_sample_kernel(pt_ref, qpad_ref, hsel_ref, qi_ref, wi_ref, knew_ref, vnew_ref, kinew_ref,
                       ck_ref, cv_ref, cki_ref, o_ref,
                       kbuf, vbuf, kibuf, sem,
                       *, n_pages, page, idx_heads, topk, total_bits, n_groups, head_dim):
    bi = pl.program_id(0)
    past = n_pages * page

    def page_copies(p):
        phys = pt_ref[bi, p]
        rows = pl.ds(pl.multiple_of(p * page, page), page)
        return (pltpu.make_async_copy(ck_ref.at[phys], kbuf.at[rows, :], sem.at[0]),
                pltpu.make_async_copy(cv_ref.at[phys], vbuf.at[rows, :], sem.at[1]),
                pltpu.make_async_copy(cki_ref.at[phys], kibuf.at[rows, :], sem.at[2]))

    def start(p, carry):
        for cp in page_copies(p):
            cp.start()
        return carry

    def wait(p, carry):
        for cp in page_copies(p):
            cp.wait()
        return carry

    lax.fori_loop(0, n_pages, start, 0)
    lax.fori_loop(0, n_pages, wait, 0)

    qi = qi_ref[0]
    wi = wi_ref[0]
    s = _dot_nt(qi, kibuf[...].astype(BF16))
    score = jnp.sum(jnp.maximum(s, 0.0) * wi, axis=0, keepdims=True) + 0.0
    s_new = jnp.sum(qi.astype(F32) * kinew_ref[0], axis=1, keepdims=True)
    score_new = jnp.sum(jnp.maximum(s_new, 0.0) * wi, axis=0, keepdims=True) + 0.0
    keys = _float_key(score)
    key_new = _float_key(score_new)
    kidx = lax.broadcasted_iota(I32, (1, past), 1)

    def count(pred):
        n = jnp.sum(pred(keys, kidx).astype(I32), axis=1, keepdims=True)
        return n + pred(key_new, jnp.full((1, 1), past, I32)).astype(I32)

    zero = jnp.zeros((1, 1), I32)
    prefix = jnp.where(count(lambda k, j: k >= zero) >= topk, zero, INT_MIN)

    def search(it, prefix):
        cand = prefix | jnp.left_shift(jnp.int32(1), 30 - it)
        return jnp.where(count(lambda k, j: k >= cand) >= topk, cand, prefix)

    thr = lax.fori_loop(0, 31, search, prefix)
    need = topk - count(lambda k, j: k > thr)

    def tie_search(it, jl):
        cand = jl | jnp.left_shift(jnp.int32(1), total_bits - 1 - it)
        n_before = count(lambda k, j: jnp.logical_and(k == thr, j < cand))
        return jnp.where(n_before < need, cand, jl)

    jlim = lax.fori_loop(0, total_bits, tie_search, zero)

    def selected(k, j):
        return jnp.logical_or(k > thr, jnp.logical_and(k == thr, j <= jlim))

    sel = selected(keys, kidx)
    sel_new = selected(key_new, jnp.full((1, 1), past, I32))

    qpad = qpad_ref[0]
    logits = jnp.where(sel, _dot_nt(qpad, kbuf[...].astype(BF16)), NEG_INF)
    logit_new = jnp.sum(qpad.astype(F32) * knew_ref[0], axis=1, keepdims=True)
    logit_new = jnp.where(sel_new, logit_new, NEG_INF)
    m = jnp.maximum(jnp.max(logits, axis=1, keepdims=True), logit_new)
    p = jnp.exp(logits - m)
    p_new = jnp.exp(logit_new - m)
    denom = jnp.sum(p, axis=1, keepdims=True) + p_new
    o_all = (_dot(p.astype(BF16), vbuf[...].astype(BF16)) + p_new * vnew_ref[0]) / denom
    out = jnp.zeros((o_all.shape[0], head_dim), F32)
    for g in range(n_groups):
        out = out + hsel_ref[g] * o_all[:, g * head_dim:(g + 1) * head_dim]
    o_ref[0] = out.astype(BF16)


def dsa_sample(q, k_new, v_new, qi, ki_new, wi, cache_k, cache_v, cache_kidx, page_table, dims):
    db, hd = q.shape
    n_pool, page, kvh, dh = cache_k.shape
    kvd = kvh * dh
    n_pages = page_table.shape[1]
    past = n_pages * page
    nh, group, ih, di = dims["n_heads"], dims["group"], dims["idx_heads"], dims["idx_dim"]
    topk = min(TOPK_MAX, (past + 1) // 4)
    head_group = jnp.arange(nh) // group
    onehot = (head_group[:, None] == jnp.arange(kvh)[None, :])
    qpad = (q.reshape(db, nh, 1, dh) * onehot[None, :, :, None].astype(q.dtype)).reshape(db, nh, kvd)
    hsel = jnp.transpose(onehot.astype(F32))[:, :, None]
    row = lambda w: pl.BlockSpec((1, 1, w), lambda bi, pt: (bi, 0, 0))
    any_spec = pl.BlockSpec(memory_space=pl.ANY)
    grid_spec = pltpu.PrefetchScalarGridSpec(
        num_scalar_prefetch=1,
        grid=(db,),
        in_specs=[
            pl.BlockSpec((1, nh, kvd), lambda bi, pt: (bi, 0, 0)),
            pl.BlockSpec((kvh, nh, 1), lambda bi, pt: (0, 0, 0)),
            pl.BlockSpec((1, ih, di), lambda bi, pt: (bi, 0, 0)),
            pl.BlockSpec((1, ih, 1), lambda bi, pt: (bi, 0, 0)),
            row(kvd), row(kvd), row(di),
            any_spec, any_spec, any_spec,
        ],
        out_specs=pl.BlockSpec((1, nh, dh), lambda bi, pt: (bi, 0, 0)),
        scratch_shapes=[pltpu.VMEM((past, kvd), F32), pltpu.VMEM((past, kvd), F32),
                        pltpu.VMEM((past, di), F32), pltpu.SemaphoreType.DMA((3,))],
    )
    out = pl.pallas_call(
        functools.partial(_dsa_sample_kernel, n_pages=n_pages, page=page, idx_heads=ih, topk=topk,
                          total_bits=max(1, past.bit_length()), n_groups=kvh, head_dim=dh),
        grid_spec=grid_spec,
        out_shape=jax.ShapeDtypeStruct((db, nh, dh), BF16),
        compiler_params=_params("arbitrary"),
        name="dsa_sample",
    )(page_table, qpad, hsel, qi.reshape(db, ih, di), wi.reshape(db, ih, 1),
      k_new.reshape(db, 1, kvd), v_new.reshape(db, 1, kvd), ki_new.reshape(db, 1, di),
      cache_k.reshape(n_pool, page, kvd), cache_v.reshape(n_pool, page, kvd), cache_kidx)
    return out.reshape(db, hd)


def _conv_prompt_kernel(x_ref, sh_ref, sc_ref, win_ref, cw_ref, a_ref, st_ref, carry_ref):
    i = pl.program_id(1)
    d = x_ref.shape[-1]
    tm = x_ref.shape[1]
    hb = _modulate(x_ref[0], sh_ref[0], sc_ref[0]).astype(BF16)
    b_gate = _dot(hb, win_ref[:, 0:d])
    z = _dot(hb, win_ref[:, d:2 * d]) * _dot(hb, win_ref[:, 2 * d:3 * d])

    @pl.when(i == 0)
    def _():
        carry_ref[...] = jnp.zeros_like(carry_ref)

    row = lax.broadcasted_iota(I32, (tm, d), 0)
    prev1 = carry_ref[SUBLANES - 1:SUBLANES, :]
    prev2 = carry_ref[SUBLANES - 2:SUBLANES - 1, :]
    z1 = jnp.where(row == 0, prev1, pltpu.roll(z, 1, axis=0))
    z2 = jnp.where(row == 0, prev2, jnp.where(row == 1, prev1, pltpu.roll(z, 2, axis=0)))
    y = cw_ref[0:1, :] * z2 + cw_ref[1:2, :] * z1 + cw_ref[2:3, :] * z
    a_ref[0] = (b_gate * y).astype(BF16)
    tail = z[tm - SUBLANES:tm, :]
    carry_ref[...] = tail
    st_ref[0] = tail


def conv_prompt(x3, shift, scale, w_in, conv_w8, tm=ROW_TILE):
    b, t, d = x3.shape
    mod_spec = pl.BlockSpec((1, 1, d), lambda bi, i: (bi, 0, 0))
    return pl.pallas_call(
        _conv_prompt_kernel,
        grid=(b, t // tm),
        in_specs=[pl.BlockSpec((1, tm, d), lambda bi, i: (bi, i, 0)), mod_spec, mod_spec,
                  pl.BlockSpec(w_in.shape, lambda bi, i: (0, 0), pipeline_mode=pl.Buffered(1)),
                  pl.BlockSpec(conv_w8.shape, lambda bi, i: (0, 0))],
        out_specs=[pl.BlockSpec((1, tm, d), lambda bi, i: (bi, i, 0)),
                   pl.BlockSpec((1, SUBLANES, d), lambda bi, i: (bi, 0, 0))],
        out_shape=[jax.ShapeDtypeStruct((b, t, d), BF16),
                   jax.ShapeDtypeStruct((b, SUBLANES, d), F32)],
        scratch_shapes=[pltpu.VMEM((SUBLANES, d), F32)],
        compiler_params=_params("arbitrary", "arbitrary"),
        name="conv_prompt",
    )(x3, shift, scale, w_in, conv_w8)


def _conv_sample_kernel(x_ref, sh_ref, sc_ref, win_ref, cw_ref, p0_ref, p1_ref, a_ref, z_ref):
    d = x_ref.shape[-1]
    hb = _modulate(x_ref[...], sh_ref[0], sc_ref[0]).astype(BF16)
    b_gate = _dot(hb, win_ref[:, 0:d])
    z = _dot(hb, win_ref[:, d:2 * d]) * _dot(hb, win_ref[:, 2 * d:3 * d])
    y = cw_ref[0:1, :] * p0_ref[...] + cw_ref[1:2, :] * p1_ref[...] + cw_ref[2:3, :] * z
    a_ref[...] = (b_gate * y).astype(BF16)
    z_ref[...] = z


def conv_sample(x, shift, scale, w_in, conv_w8, prefix0, prefix1):
    n, d = x.shape
    full = pl.BlockSpec((n, d), lambda i: (0, 0))
    mod_spec = pl.BlockSpec((1, n, d), lambda i: (0, 0, 0))
    return pl.pallas_call(
        _conv_sample_kernel,
        grid=(1,),
        in_specs=[full, mod_spec, mod_spec,
                  pl.BlockSpec(w_in.shape, lambda i: (0, 0), pipeline_mode=pl.Buffered(1)),
                  pl.BlockSpec(conv_w8.shape, lambda i: (0, 0)), full, full],
        out_specs=[full, full],
        out_shape=[jax.ShapeDtypeStruct((n, d), BF16), jax.ShapeDtypeStruct((n, d), F32)],
        compiler_params=_params("arbitrary"),
        name="conv_sample",
    )(x, shift, scale, w_in, conv_w8, prefix0, prefix1)


def kernel(x_prompt, x_sample, cache_k, cache_v, cache_kidx, state_conv, page_table, c_prompt, c_sample, w_ada, b_ada, w_ffn_in, w_ffn_out, w_attn_in, w_attn_out, q_norm_gain, k_norm_gain, w_conv_in, conv_w, w_conv_out):
    b, t, d = x_prompt.shape
    db, dt, _ = x_sample.shape
    assert dt == 1, "the sample path handles one new token per sequence"
    depth = w_ada.shape[0]
    head_dim = q_norm_gain.shape[-1]
    n_heads = w_attn_out.shape[1] // head_dim
    kvh = cache_k.shape[3]
    idx_dim = cache_kidx.shape[-1]
    hd, kvd = n_heads * head_dim, kvh * head_dim
    idx_heads = (w_attn_in.shape[-1] - hd - 2 * kvd - idx_dim) // (idx_dim + 1)
    ihd = idx_heads * idx_dim
    conv_width = conv_w.shape[1]
    assert conv_width == 3
    dims = dict(n_heads=n_heads, group=n_heads // kvh, head_dim=head_dim, hd=hd, kvd=kvd, ihd=ihd,
                idx_dim=idx_dim, idx_heads=idx_heads, qk_scale=head_dim ** -0.5,
                wi_scale=idx_heads ** -0.5 * idx_dim ** -0.5)

    n_c = b + db
    n_c_pad = -(-n_c // SUBLANES) * SUBLANES
    c_all = jnp.concatenate([c_prompt, c_sample, jnp.zeros((n_c_pad - n_c, d), F32)], axis=0)
    mods = ada_mods(c_all, w_ada, b_ada).reshape(depth, n_c_pad, N_MOD, d)

    def prompt_mod(layer, j):
        return mods[layer, :b, j][:, None, :]

    def sample_mod(layer, j):
        return mods[layer, b:n_c, j][None]

    gw = 256 if kvd % 256 == 0 and hd % 256 == 0 else kvd
    gi = jnp.arange(gw) // head_dim
    g_mat = jnp.where(gi[:, None] == gi[None, :], 1.0 / head_dim, 0.0).astype(BF16)

    xp = x_prompt.reshape(b * t, d)
    xs = x_sample.reshape(db, d)
    pk, pv, pki, pconv, sk, sv, ski, sconv = [], [], [], [], [], [], [], []
    n_mixers = 2
    for layer in range(depth):
        li = layer // n_mixers
        w1_in, w1_out = w_ffn_in[layer, 0].astype(BF16), w_ffn_out[layer, 0].astype(BF16)
        w2_in, w2_out = w_ffn_in[layer, 1].astype(BF16), w_ffn_out[layer, 1].astype(BF16)
        pm = [prompt_mod(layer, j) for j in range(N_MOD)]
        sm = [sample_mod(layer, j) for j in range(N_MOD)]
        xp = ffn_call(xp, pm[0], pm[1], pm[2], w1_in, w1_out)
        xs = ffn_call(xs, sm[0], sm[1], sm[2], w1_in, w1_out)
        if layer % n_mixers == 0:
            wa = w_attn_in[li].astype(BF16)
            o1, o2, o3, o4, o5 = hd, hd + kvd, hd + 2 * kvd, hd + 2 * kvd + ihd, hd + 2 * kvd + ihd + idx_dim
            wkw = jnp.pad(wa[:, o4:], ((0, 0), (0, 128 - (idx_dim + idx_heads))))
            aw = dict(wq=wa[:, :o1], wk=wa[:, o1:o2], wv=wa[:, o2:o3], wqi=wa[:, o3:o4], wkw=wkw,
                      gq=jnp.tile(q_norm_gain[li], n_heads)[None, :],
                      gk=jnp.tile(k_norm_gain[li], kvh)[None, :], g=g_mat)
            w_mix_out = w_attn_out[li].astype(BF16)
            qt, k, kb, v, vt, qit, ki, kib, wit = attn_proj_prompt(xp.reshape(b, t, d), pm[3], pm[4], aw, dims)
            ap = dsa_prompt(qt, qit, wit, kb, vt, kib, dims).reshape(b * t, hd)
            pk.append(k.reshape(b, t, kvh, head_dim))
            pv.append(v.reshape(b, t, kvh, head_dim))
            pki.append(ki)
            qs, ks, vs, qis, kis, wis = attn_proj_sample(xs, sm[3], sm[4], aw, dims)
            a_s = dsa_sample(qs, ks, vs, qis, kis, wis, cache_k[li], cache_v[li], cache_kidx[li],
                             page_table, dims)
            sk.append(ks.reshape(db, 1, kvh, head_dim))
            sv.append(vs.reshape(db, 1, kvh, head_dim))
            ski.append(kis.reshape(db, 1, idx_dim))
        else:
            wc_in = w_conv_in[li].astype(BF16)
            w_mix_out = w_conv_out[li].astype(BF16)
            cw8 = jnp.pad(conv_w[li], ((0, SUBLANES - conv_width), (0, 0)))
            ap, st = conv_prompt(xp.reshape(b, t, d), pm[3], pm[4], wc_in, cw8)
            ap = ap.reshape(b * t, d)
            pconv.append(st[:, SUBLANES - (conv_width - 1):, :])
            prefix = state_conv[li]
            a_s, zs = conv_sample(xs, sm[3], sm[4], wc_in, cw8, prefix[:, 0], prefix[:, 1])
            sconv.append(jnp.stack([prefix[:, 1], zs], axis=1))
        xp = ffn_call(xp, pm[6], pm[7], pm[8], w2_in, w2_out, pre=(ap, w_mix_out, pm[5]))
        xs = ffn_call(xs, sm[6], sm[7], sm[8], w2_in, w2_out, pre=(a_s, w_mix_out, sm[5]))

    return (xp.reshape(b, t, d), xs.reshape(db, 1, d),
            jnp.stack(pk), jnp.stack(pv), jnp.stack(pki), jnp.stack(pconv),
            jnp.stack(sk), jnp.stack(sv), jnp.stack(ski), jnp.stack(sconv))
```

```python
import functools

import jax
import jax.numpy as jnp
from jax import lax
from jax.experimental import pallas as pl
from jax.experimental.pallas import tpu as pltpu

F32 = jnp.float32
BF16 = jnp.bfloat16
I32 = jnp.int32
I16 = jnp.int16

RMS_EPS = 1e-6
TOPK_MAX = 256
N_MOD = 9
SUBLANES = 8
BF16_SUBLANES = 16
LOG2_E = 1.4426950408889634
NEG_INF = float("-inf")
INT_MIN = -(2 ** 31)
INT16_MIN = -(2 ** 15)
HALF16 = 2 ** 15
KEY_NEG_INF = (0xFF800000 ^ 0x7FFFFFFF) - (1 << 32)

VMEM_LIMIT_BYTES = 56 * 1024 * 1024
ROW_TILE = 256
FFN_ROW_TILE = 512
FF_CHUNK = 256


def _params(*sem):
    return pltpu.CompilerParams(dimension_semantics=sem, vmem_limit_bytes=VMEM_LIMIT_BYTES)


def _sigmoid(x):
    return 1.0 / (1.0 + jnp.exp(-x))


def _modulate(x, shift, scale):
    ms = jnp.mean(x * x, axis=-1, keepdims=True)
    return (x * lax.rsqrt(ms + RMS_EPS)) * (1.0 + scale) + shift


def _dot(a, b):
    return jnp.dot(a, b, preferred_element_type=F32)


def _dot_nt(a, b):
    return lax.dot_general(a, b, (((1,), (1,)), ((), ())), preferred_element_type=F32)


def _float_key(x):
    bits = lax.bitcast_convert_type(x, I32)
    return bits ^ (lax.shift_right_arithmetic(bits, 31) & 0x7FFFFFFF)


def _fold_rows(x, op, rows=SUBLANES):
    acc = x[0:rows]
    for j in range(1, x.shape[0] // rows):
        acc = op(acc, x[j * rows:(j + 1) * rows])
    return acc


def _ada_kernel(c_ref, w_ref, b_ref, o_ref):
    c = c_ref[...]
    s = (c * _sigmoid(c)).astype(BF16)
    o_ref[0] = _dot(s, w_ref[0].astype(BF16)) + b_ref[0]


def ada_mods(c_all, w_ada, b_ada):
    depth, d, nd = w_ada.shape
    r = c_all.shape[0]
    return pl.pallas_call(
        _ada_kernel,
        grid=(depth, nd // d),
        in_specs=[
            pl.BlockSpec((r, d), lambda l, j: (0, 0)),
            pl.BlockSpec((1, d, d), lambda l, j: (l, 0, j)),
            pl.BlockSpec((1, 1, d), lambda l, j: (l, 0, j)),
        ],
        out_specs=pl.BlockSpec((1, r, d), lambda l, j: (l, 0, j)),
        out_shape=jax.ShapeDtypeStruct((depth, r, nd), F32),
        compiler_params=_params("arbitrary", "arbitrary"),
        name="ada_mods",
    )(c_all, w_ada, b_ada.reshape(depth, 1, nd))


def _ffn_kernel(*refs, has_pre, d_ff, tf):
    if has_pre:
        (x_ref, a_ref, wpre_ref, gpre_ref, sh_ref, sc_ref, g_ref, win_ref, wout_ref,
         o_ref, acc_ref) = refs
    else:
        x_ref, sh_ref, sc_ref, g_ref, win_ref, wout_ref, o_ref, acc_ref = refs
    x = x_ref[...]
    if has_pre:
        x = x + gpre_ref[0] * _dot(a_ref[...], wpre_ref[...])
    hb = _modulate(x, sh_ref[0], sc_ref[0]).astype(BF16)
    for c in range(d_ff // tf):
        u = _dot(hb, win_ref[:, c * tf:(c + 1) * tf])
        g = _dot(hb, win_ref[:, d_ff + c * tf:d_ff + (c + 1) * tf])
        act = ((u * _sigmoid(u)) * g).astype(BF16)
        y = _dot(act, wout_ref[c * tf:(c + 1) * tf, :])
        if c == 0:
            acc_ref[...] = y
        else:
            acc_ref[...] += y
    o_ref[...] = x + 0.5 * g_ref[0] * acc_ref[...]


def ffn_call(x, shift, scale, gate, w_in, w_out, pre=None, tm=FFN_ROW_TILE):
    n, d = x.shape
    tm = min(tm, n)
    groups = shift.shape[0]
    tiles_per_group = n // tm // groups
    d_ff = w_out.shape[0]
    tf = FF_CHUNK if d_ff % FF_CHUNK == 0 else d_ff

    def row_spec(width):
        return pl.BlockSpec((tm, width), lambda i: (i, 0))

    def mod_spec(m):
        return pl.BlockSpec((1,) + m.shape[1:], lambda i: (i // tiles_per_group, 0, 0))

    def whole(w):
        return pl.BlockSpec(w.shape, lambda i: (0, 0), pipeline_mode=pl.Buffered(1))

    args, specs = [x], [row_spec(d)]
    if pre is not None:
        a, w_pre, g_pre = pre
        args += [a, w_pre, g_pre]
        specs += [row_spec(a.shape[1]), whole(w_pre), mod_spec(g_pre)]
    args += [shift, scale, gate, w_in, w_out]
    specs += [mod_spec(shift), mod_spec(scale), mod_spec(gate), whole(w_in), whole(w_out)]
    return pl.pallas_call(
        functools.partial(_ffn_kernel, has_pre=pre is not None, d_ff=d_ff, tf=tf),
        grid=(n // tm,),
        in_specs=specs,
        out_specs=row_spec(d),
        out_shape=jax.ShapeDtypeStruct((n, d), F32),
        scratch_shapes=[pltpu.VMEM((tm, d), F32)],
        compiler_params=_params("arbitrary"),
        name="ffn_pre" if pre is not None else "ffn",
    )(*args)


def _head_mean_square(y, g_ref):
    y2 = (y * y).astype(BF16)
    w = g_ref.shape[0]
    parts = [_dot(y2[:, j * w:(j + 1) * w], g_ref[...]) for j in range(y.shape[1] // w)]
    return parts[0] if len(parts) == 1 else jnp.concatenate(parts, axis=1)


def _project(x, sh, sc, wq_ref, wk_ref, wv_ref, wqi_ref, wkw_ref, gq_ref, gk_ref, g_ref,
             qk_scale):
    hb = _modulate(x, sh, sc).astype(BF16)
    q = _dot(hb, wq_ref[...])
    qn = q * lax.rsqrt(_head_mean_square(q, g_ref) + RMS_EPS) * gq_ref[...] * qk_scale
    k = _dot(hb, wk_ref[...])
    kn = k * lax.rsqrt(_head_mean_square(k, g_ref) + RMS_EPS) * gk_ref[...]
    v = _dot(hb, wv_ref[...])
    qi = _dot(hb, wqi_ref[...])
    kw = _dot(hb, wkw_ref[...])
    return qn, kn, v, qi, kw


def _proj_kernel(x_ref, sh_ref, sc_ref, wq_ref, wk_ref, wv_ref, wqi_ref, wkw_ref, gq_ref,
                 gk_ref, g_ref,
                 qt_ref, k_ref, kb_ref, v_ref, vt_ref, qit_ref, ki_ref, kib_ref, wit_ref,
                 *, idx_dim, idx_heads, head_dim, v_rows, qk_scale, wi_scale):
    qn, kn, v, qi, kw = _project(x_ref[0], sh_ref[0], sc_ref[0], wq_ref, wk_ref, wv_ref,
                                 wqi_ref, wkw_ref, gq_ref, gk_ref, g_ref, qk_scale)
    tm = x_ref.shape[1]
    qt_ref[0, 0] = qn.T.astype(BF16)
    k_ref[0] = kn
    kb_ref[0] = kn.astype(BF16)
    v_ref[0] = v
    vt = v.T.astype(BF16)
    ones_rows = (lax.broadcasted_iota(I32, (v_rows - head_dim, tm), 0) == 0).astype(BF16)
    for g in range(v.shape[1] // head_dim):
        vt_ref[0, 0, g * v_rows:g * v_rows + head_dim, :] = vt[g * head_dim:(g + 1) * head_dim]
        vt_ref[0, 0, g * v_rows + head_dim:(g + 1) * v_rows, :] = ones_rows
    qit_ref[0, 0] = qi.T.astype(BF16)
    ki = kw[:, :idx_dim]
    ki_ref[0] = ki
    kib_ref[0] = ki.astype(BF16)
    wit_ref[0, 0] = kw.T[idx_dim:idx_dim + idx_heads, :] * wi_scale


def _sample_proj_kernel(x_ref, sh_ref, sc_ref, wq_ref, wk_ref, wv_ref, wqi_ref, wkw_ref,
                        gq_ref, gk_ref, g_ref,
                        q_ref, k_ref, v_ref, qi_ref, ki_ref, wi_ref,
                        *, idx_dim, idx_heads, qk_scale, wi_scale):
    qn, kn, v, qi, kw = _project(x_ref[...], sh_ref[0], sc_ref[0], wq_ref, wk_ref, wv_ref,
                                 wqi_ref, wkw_ref, gq_ref, gk_ref, g_ref, qk_scale)
    q_ref[...] = qn.astype(BF16)
    k_ref[...] = kn
    v_ref[...] = v
    qi_ref[...] = qi.astype(BF16)
    ki_ref[...] = kw[:, :idx_dim]
    wi_ref[...] = kw[:, idx_dim:idx_dim + idx_heads] * wi_scale


def _proj_weight_specs(ws, n_grid):
    zeros = (lambda *idx: (0, 0))
    return [pl.BlockSpec(w.shape, zeros, pipeline_mode=pl.Buffered(1)) for w in ws]


def attn_proj_prompt(x3, shift, scale, aw, dims, tm=ROW_TILE):
    b, t, d = x3.shape
    nt = t // tm
    hd, kvd, ihd, di, ih = dims["hd"], dims["kvd"], dims["ihd"], dims["idx_dim"], dims["idx_heads"]
    ws = [aw["wq"], aw["wk"], aw["wv"], aw["wqi"], aw["wkw"], aw["gq"], aw["gk"], aw["g"]]
    mod_spec = pl.BlockSpec((1, 1, d), lambda bi, i: (bi, 0, 0))
    nat = lambda w: pl.BlockSpec((1, tm, w), lambda bi, i: (bi, i, 0))
    tr = lambda r: pl.BlockSpec((1, 1, r, tm), lambda bi, i: (bi, i, 0, 0))
    vtd = kvd // dims["head_dim"] * dims["v_rows"]
    return pl.pallas_call(
        functools.partial(_proj_kernel, idx_dim=di, idx_heads=ih, head_dim=dims["head_dim"],
                          v_rows=dims["v_rows"], qk_scale=dims["qk_scale"] * LOG2_E,
                          wi_scale=dims["wi_scale"]),
        grid=(b, nt),
        in_specs=[nat(d), mod_spec, mod_spec] + _proj_weight_specs(ws, 2),
        out_specs=[tr(hd), nat(kvd), nat(kvd), nat(kvd), tr(vtd), tr(ihd), nat(di), nat(di), tr(ih)],
        out_shape=[
            jax.ShapeDtypeStruct((b, nt, hd, tm), BF16),
            jax.ShapeDtypeStruct((b, t, kvd), F32),
            jax.ShapeDtypeStruct((b, t, kvd), BF16),
            jax.ShapeDtypeStruct((b, t, kvd), F32),
            jax.ShapeDtypeStruct((b, nt, vtd, tm), BF16),
            jax.ShapeDtypeStruct((b, nt, ihd, tm), BF16),
            jax.ShapeDtypeStruct((b, t, di), F32),
            jax.ShapeDtypeStruct((b, t, di), BF16),
            jax.ShapeDtypeStruct((b, nt, ih, tm), F32),
        ],
        compiler_params=_params("arbitrary", "arbitrary"),
        name="attn_proj",
    )(x3, shift, scale, *ws)


def attn_proj_sample(x, shift, scale, aw, dims):
    n, d = x.shape
    hd, kvd, ihd, di, ih = dims["hd"], dims["kvd"], dims["ihd"], dims["idx_dim"], dims["idx_heads"]
    ws = [aw["wq"], aw["wk"], aw["wv"], aw["wqi"], aw["wkw"], aw["gq"], aw["gk"], aw["g"]]
    full = lambda w: pl.BlockSpec((n, w), lambda i: (0, 0))
    mod_spec = pl.BlockSpec((1, n, d), lambda i: (0, 0, 0))
    return pl.pallas_call(
        functools.partial(_sample_proj_kernel, idx_dim=di, idx_heads=ih,
                          qk_scale=dims["qk_scale"], wi_scale=dims["wi_scale"]),
        grid=(1,),
        in_specs=[full(d), mod_spec, mod_spec] + _proj_weight_specs(ws, 1),
        out_specs=[full(hd), full(kvd), full(kvd), full(ihd), full(di), full(ih)],
        out_shape=[
            jax.ShapeDtypeStruct((n, hd), BF16),
            jax.ShapeDtypeStruct((n, kvd), F32),
            jax.ShapeDtypeStruct((n, kvd), F32),
            jax.ShapeDtypeStruct((n, ihd), BF16),
            jax.ShapeDtypeStruct((n, di), F32),
            jax.ShapeDtypeStruct((n, ih), F32),
        ],
        compiler_params=_params("arbitrary"),
        name="attn_proj_sample",
    )(x, shift, scale, *ws)


def _dsa_kernel(qt_ref, qit_ref, wit_ref, k_ref, vt_ref, ki_ref, o_ref,
                keys_ref, hi_ref, lo_ref, qpad_ref, s_ref, acc_ref, m_ref, mnew_ref, l_ref, jlim_ref,
                *, n_heads, group, head_dim, v_rows, idx_heads, idx_dim, topk, seq_bits):
    tq = qt_ref.shape[-1]
    tk = tq
    tk2 = 2 * tk
    i = pl.program_id(1)
    n_chunks = i + 1
    n_pairs = (n_chunks + 1) // 2
    q_idx = i * tq + lax.broadcasted_iota(I32, (tk, tq), 1)
    row_iota = lax.broadcasted_iota(I32, (tk, tq), 0)
    q_idx2 = i * tq + lax.broadcasted_iota(I32, (tk2, tq), 1)
    row_iota2 = lax.broadcasted_iota(I32, (tk2, tq), 0)

    def chunk_rows(c):
        return pl.ds(pl.multiple_of(c * tk, tk), tk)

    def pair_rows(j):
        return pl.ds(pl.multiple_of(j * tk2, tk2), tk2)

    def score_chunk(c, carry):
        kic = ki_ref[0, chunk_rows(c), :]
        acc = jnp.zeros((tk, tq), F32)
        for h in range(idx_heads):
            s = _dot(kic, qit_ref[0, 0, h * idx_dim:(h + 1) * idx_dim, :])
            acc = acc + jnp.maximum(s, 0.0) * wit_ref[0, 0, h:h + 1, :]
        acc = jnp.where(c * tk + row_iota <= q_idx, acc, NEG_INF)
        key = _float_key(acc)
        keys_ref[chunk_rows(c), :] = key
        hi_ref[chunk_rows(c), :] = lax.shift_right_arithmetic(key, 16).astype(I16)
        lo_ref[chunk_rows(c), :] = ((key & 0xFFFF) - HALF16).astype(I16)
        return carry

    lax.fori_loop(0, n_chunks, score_chunk, 0)

    @pl.when(n_chunks % 2 == 1)
    def _():
        keys_ref[chunk_rows(n_chunks), :] = jnp.full((tk, tq), INT_MIN, I32)
        hi_ref[chunk_rows(n_chunks), :] = jnp.full((tk, tq), INT16_MIN, I16)
        lo_ref[chunk_rows(n_chunks), :] = jnp.full((tk, tq), INT16_MIN, I16)

    def count(pred):
        def body(j, cnt):
            kidx = j * tk2 + row_iota2
            return cnt + _fold_rows(pred(keys_ref[pair_rows(j), :], kidx).astype(I32), jnp.add)
        cnt8 = lax.fori_loop(0, n_pairs, body, jnp.zeros((SUBLANES, tq), I32))
        return jnp.sum(cnt8, axis=0, keepdims=True)

    def count_ge16(ref, cand):
        cand16 = cand.astype(I16)
        def body(j, cnt):
            m = (ref[pair_rows(j), :] >= cand16).astype(I16)
            return cnt + _fold_rows(m, jnp.add, BF16_SUBLANES).astype(I32)
        cnt16 = lax.fori_loop(0, n_pairs, body, jnp.zeros((BF16_SUBLANES, tq), I32))
        return jnp.sum(cnt16, axis=0, keepdims=True)

    def bisect16(ref, k_need, n_all):
        zero = jnp.zeros((1, tq), I32)
        c0 = count_ge16(ref, zero)
        nonneg = c0 >= k_need
        init = (jnp.where(nonneg, zero, INT16_MIN), jnp.where(nonneg, c0, n_all),
                jnp.where(nonneg, zero, c0))

        def search(it, carry):
            prefix, n_ge, n_up = carry
            cand = prefix | jnp.left_shift(jnp.int32(1), 14 - it)
            c = count_ge16(ref, cand)
            ok = c >= k_need
            return jnp.where(ok, cand, prefix), jnp.where(ok, c, n_ge), jnp.where(ok, n_up, c)

        return lax.fori_loop(0, 15, search, init)

    zero = jnp.zeros((1, tq), I32)
    thr_hi, n_ge_hi, n_gt_hi = bisect16(hi_ref, topk, n_chunks * tk)

    thr_hi16 = thr_hi.astype(I16)

    def mask_low(j, carry):
        keep = hi_ref[pair_rows(j), :] == thr_hi16
        lo_ref[pair_rows(j), :] = jnp.where(keep, lo_ref[pair_rows(j), :], INT16_MIN).astype(I16)
        return carry

    lax.fori_loop(0, n_pairs, mask_low, 0)
    thr_lo, n_ge_lo, n_gt_lo = bisect16(lo_ref, topk - n_gt_hi, n_ge_hi - n_gt_hi)
    thr = lax.shift_left(thr_hi, 16) | (thr_lo + HALF16)
    n_ge = n_gt_hi + n_ge_lo
    n_gt = n_gt_hi + n_gt_lo
    need = topk - n_gt

    jlim_ref[...] = jnp.full((1, tq), (1 << seq_bits) - 1, I32)
    excess = jnp.logical_and(n_ge > topk, thr > KEY_NEG_INF)

    @pl.when(jnp.max(excess.astype(I32)) > 0)
    def _():
        def tie_search(it, j):
            cand = j | jnp.left_shift(jnp.int32(1), seq_bits - 1 - it)
            n_before = count(lambda kc, kidx: jnp.logical_and(kc == thr, kidx < cand))
            return jnp.where(n_before < need, cand, j)
        jlim_ref[...] = lax.fori_loop(0, seq_bits, tie_search, zero)

    jlim = jlim_ref[...]

    def bias_pair(j, carry):
        kc = keys_ref[pair_rows(j), :]
        kidx = j * tk2 + row_iota2
        sel = jnp.logical_or(kc > thr, jnp.logical_and(kc == thr, kidx <= jlim))
        sel = jnp.logical_and(sel, kidx <= q_idx2)
        bias = jnp.where(sel, 0.0, NEG_INF).astype(F32)
        keys_ref[pair_rows(j), :] = lax.bitcast_convert_type(bias, I32)
        return carry

    lax.fori_loop(0, n_pairs, bias_pair, 0)

    qpad_ref[...] = jnp.zeros(qpad_ref.shape, BF16)
    for h in range(n_heads):
        g = h // group
        qpad_ref[h, g * head_dim:(g + 1) * head_dim, :] = qt_ref[0, 0, h * head_dim:(h + 1) * head_dim, :]
    m_ref[...] = jnp.full(m_ref.shape, NEG_INF, F32)
    l_ref[...] = jnp.zeros(l_ref.shape, F32)
    acc_ref[...] = jnp.zeros(acc_ref.shape, F32)

    def attend(c, carry):
        kc = k_ref[0, chunk_rows(c), :]
        bias = lax.bitcast_convert_type(keys_ref[chunk_rows(c), :], F32)
        for h in range(n_heads):
            s = _dot(kc, qpad_ref[h]) + bias
            s_ref[h] = s
            mnew_ref[h:h + 1, :] = jnp.maximum(
                m_ref[h:h + 1, :], jnp.max(_fold_rows(s, jnp.maximum), axis=0, keepdims=True))
        for h in range(n_heads):
            g = h // group
            rows = slice(h * head_dim, (h + 1) * head_dim)
            m_old = m_ref[h:h + 1, :]
            m_new = mnew_ref[h:h + 1, :]
            m_safe = jnp.where(m_new == NEG_INF, 0.0, m_new)
            p = jnp.exp2(s_ref[h] - m_safe).astype(BF16)
            alpha = jnp.exp2(m_old - m_safe)
            pv = _dot(vt_ref[0, c, g * v_rows:(g + 1) * v_rows, :], p)
            acc_ref[rows, :] = alpha * acc_ref[rows, :] + pv[0:head_dim]
            l_ref[h:h + 1, :] = alpha * l_ref[h:h + 1, :] + pv[head_dim:head_dim + 1]
            m_ref[h:h + 1, :] = m_new
        return carry

    lax.fori_loop(0, n_chunks, attend, 0)
    for h in range(n_heads):
        rows = slice(h * head_dim, (h + 1) * head_dim)
        acc_ref[rows, :] = acc_ref[rows, :] / l_ref[h:h + 1, :]
    o_ref[0] = acc_ref[...].T.astype(BF16)


def dsa_prompt(qt, qit, wit, kb, vt, kib, dims):
    b, nt, hd, tq = qt.shape
    t = nt * tq
    kvd, ihd, ih = kb.shape[-1], qit.shape[2], wit.shape[2]
    nh = dims["n_heads"]
    topk = min(TOPK_MAX, t // 4)
    blk = lambda r: pl.BlockSpec((1, 1, r, tq), lambda bi, i: (bi, i, 0, 0))
    per_batch3 = lambda w: pl.BlockSpec((1, t, w), lambda bi, i: (bi, 0, 0),
                                        pipeline_mode=pl.Buffered(1))
    return pl.pallas_call(
        functools.partial(_dsa_kernel, n_heads=nh, group=dims["group"],
                          head_dim=dims["head_dim"], v_rows=dims["v_rows"], idx_heads=ih,
                          idx_dim=dims["idx_dim"], topk=topk, seq_bits=max(1, (t - 1).bit_length())),
        grid=(b, nt),
        in_specs=[blk(hd), blk(ihd), blk(ih), per_batch3(kvd),
                  pl.BlockSpec((1, nt) + vt.shape[2:], lambda bi, i: (bi, 0, 0, 0),
                               pipeline_mode=pl.Buffered(1)),
                  per_batch3(dims["idx_dim"])],
        out_specs=pl.BlockSpec((1, tq, hd), lambda bi, i: (bi, i, 0)),
        out_shape=jax.ShapeDtypeStruct((b, t, hd), BF16),
        scratch_shapes=[pltpu.VMEM((t + tq, tq), I32), pltpu.VMEM((t + tq, tq), I16),
                        pltpu.VMEM((t + tq, tq), I16), pltpu.VMEM((nh, kvd, tq), BF16),
                        pltpu.VMEM((nh, tq, tq), F32), pltpu.VMEM((hd, tq), F32),
                        pltpu.VMEM((nh, tq), F32), pltpu.VMEM((nh, tq), F32),
                        pltpu.VMEM((nh, tq), F32), pltpu.VMEM((1, tq), I32)],
        compiler_params=_params("arbitrary", "arbitrary"),
        name="dsa_prompt",
    )(qt, qit, wit, kb, vt, kib)


def _dsa_sample_kernel(pt_ref, qpad_ref, hsel_ref, qi_ref, wi_ref, knew_ref, vnew_ref, kinew_ref,
                       ck_ref, cv_ref, cki_ref, o_ref,
                       kbuf, vbuf, kibuf, sem,
                       *, layer, n_pages, page, idx_heads, topk, total_bits, n_groups, head_dim):
    bi = pl.program_id(0)
    past = n_pages * page

    def page_copies(p):
        phys = pt_ref[bi, p]
        rows = pl.ds(pl.multiple_of(p * page, page), page)
        return (pltpu.make_async_copy(ck_ref.at[layer, phys], kbuf.at[rows, :], sem.at[0]),
                pltpu.make_async_copy(cv_ref.at[layer, phys], vbuf.at[rows, :], sem.at[1]),
                pltpu.make_async_copy(cki_ref.at[layer, phys], kibuf.at[rows, :], sem.at[2]))

    def start(p, carry):
        for cp in page_copies(p):
            cp.start()
        return carry

    def wait(p, carry):
        for cp in page_copies(p):
            cp.wait()
        return carry

    lax.fori_loop(0, n_pages, start, 0)
    lax.fori_loop(0, n_pages, wait, 0)

    qi = qi_ref[0]
    wi = wi_ref[0]
    s = _dot_nt(qi, kibuf[...].astype(BF16))
    score = jnp.sum(jnp.maximum(s, 0.0) * wi, axis=0, keepdims=True) + 0.0
    s_new = jnp.sum(qi.astype(F32) * kinew_ref[0], axis=1, keepdims=True)
    score_new = jnp.sum(jnp.maximum(s_new, 0.0) * wi, axis=0, keepdims=True) + 0.0
    keys = _float_key(score)
    key_new = _float_key(score_new)
    kidx = lax.broadcasted_iota(I32, (1, past), 1)

    def count(pred):
        n = jnp.sum(pred(keys, kidx).astype(I32), axis=1, keepdims=True)
        return n + pred(key_new, jnp.full((1, 1), past, I32)).astype(I32)

    zero = jnp.zeros((1, 1), I32)
    prefix = jnp.where(count(lambda k, j: k >= zero) >= topk, zero, INT_MIN)

    def search(it, prefix):
        cand = prefix | jnp.left_shift(jnp.int32(1), 30 - it)
        return jnp.where(count(lambda k, j: k >= cand) >= topk, cand, prefix)

    thr = lax.fori_loop(0, 31, search, prefix)
    need = topk - count(lambda k, j: k > thr)

    def tie_search(it, jl):
        cand = jl | jnp.left_shift(jnp.int32(1), total_bits - 1 - it)
        n_before = count(lambda k, j: jnp.logical_and(k == thr, j < cand))
        return jnp.where(n_before < need, cand, jl)

    jlim = lax.fori_loop(0, total_bits, tie_search, zero)

    def selected(k, j):
        return jnp.logical_or(k > thr, jnp.logical_and(k == thr, j <= jlim))

    sel = selected(keys, kidx)
    sel_new = selected(key_new, jnp.full((1, 1), past, I32))

    qpad = qpad_ref[0]
    logits = jnp.where(sel, _dot_nt(qpad, kbuf[...].astype(BF16)), NEG_INF)
    logit_new = jnp.sum(qpad.astype(F32) * knew_ref[0], axis=1, keepdims=True)
    logit_new = jnp.where(sel_new, logit_new, NEG_INF)
    m = jnp.maximum(jnp.max(logits, axis=1, keepdims=True), logit_new)
    p = jnp.exp(logits - m)
    p_new = jnp.exp(logit_new - m)
    denom = jnp.sum(p, axis=1, keepdims=True) + p_new
    o_all = (_dot(p.astype(BF16), vbuf[...].astype(BF16)) + p_new * vnew_ref[0]) / denom
    out = jnp.zeros((o_all.shape[0], head_dim), F32)
    for g in range(n_groups):
        out = out + hsel_ref[g] * o_all[:, g * head_dim:(g + 1) * head_dim]
    o_ref[0] = out.astype(BF16)


def dsa_sample(q, k_new, v_new, qi, ki_new, wi, cache_k, cache_v, cache_kidx, layer, page_table, dims):
    db, hd = q.shape
    n_layers, n_pool, page, kvh, dh = cache_k.shape
    kvd = kvh * dh
    n_pages = page_table.shape[1]
    past = n_pages * page
    nh, group, ih, di = dims["n_heads"], dims["group"], dims["idx_heads"], dims["idx_dim"]
    topk = min(TOPK_MAX, (past + 1) // 4)
    head_group = jnp.arange(nh) // group
    onehot = (head_group[:, None] == jnp.arange(kvh)[None, :])
    qpad = (q.reshape(db, nh, 1, dh) * onehot[None, :, :, None].astype(q.dtype)).reshape(db, nh, kvd)
    hsel = jnp.transpose(onehot.astype(F32))[:, :, None]
    row = lambda w: pl.BlockSpec((1, 1, w), lambda bi, pt: (bi, 0, 0))
    any_spec = pl.BlockSpec(memory_space=pl.ANY)
    grid_spec = pltpu.PrefetchScalarGridSpec(
        num_scalar_prefetch=1,
        grid=(db,),
        in_specs=[
            pl.BlockSpec((1, nh, kvd), lambda bi, pt: (bi, 0, 0)),
            pl.BlockSpec((kvh, nh, 1), lambda bi, pt: (0, 0, 0)),
            pl.BlockSpec((1, ih, di), lambda bi, pt: (bi, 0, 0)),
            pl.BlockSpec((1, ih, 1), lambda bi, pt: (bi, 0, 0)),
            row(kvd), row(kvd), row(di),
            any_spec, any_spec, any_spec,
        ],
        out_specs=pl.BlockSpec((1, nh, dh), lambda bi, pt: (bi, 0, 0)),
        scratch_shapes=[pltpu.VMEM((past, kvd), F32), pltpu.VMEM((past, kvd), F32),
                        pltpu.VMEM((past, di), F32), pltpu.SemaphoreType.DMA((3,))],
    )
    out = pl.pallas_call(
        functools.partial(_dsa_sample_kernel, layer=layer, n_pages=n_pages, page=page, idx_heads=ih, topk=topk,
                          total_bits=max(1, past.bit_length()), n_groups=kvh, head_dim=dh),
        grid_spec=grid_spec,
        out_shape=jax.ShapeDtypeStruct((db, nh, dh), BF16),
        compiler_params=_params("arbitrary"),
        name="dsa_sample",
    )(page_table, qpad, hsel, qi.reshape(db, ih, di), wi.reshape(db, ih, 1),
      k_new.reshape(db, 1, kvd), v_new.reshape(db, 1, kvd), ki_new.reshape(db, 1, di),
      cache_k.reshape(n_layers, n_pool, page, kvd), cache_v.reshape(n_layers, n_pool, page, kvd),
      cache_kidx)
    return out.reshape(db, hd)


def _conv_prompt_kernel(x_ref, sh_ref, sc_ref, win_ref, cw_ref, a_ref, st_ref, carry_ref):
    i = pl.program_id(1)
    d = x_ref.shape[-1]
    tm = x_ref.shape[1]
    hb = _modulate(x_ref[0], sh_ref[0], sc_ref[0]).astype(BF16)
    b_gate = _dot(hb, win_ref[:, 0:d])
    z = _dot(hb, win_ref[:, d:2 * d]) * _dot(hb, win_ref[:, 2 * d:3 * d])

    @pl.when(i == 0)
    def _():
        carry_ref[...] = jnp.zeros_like(carry_ref)

    row = lax.broadcasted_iota(I32, (tm, d), 0)
    prev1 = carry_ref[SUBLANES - 1:SUBLANES, :]
    prev2 = carry_ref[SUBLANES - 2:SUBLANES - 1, :]
    z1 = jnp.where(row == 0, prev1, pltpu.roll(z, 1, axis=0))
    z2 = jnp.where(row == 0, prev2, jnp.where(row == 1, prev1, pltpu.roll(z, 2, axis=0)))
    y = cw_ref[0:1, :] * z2 + cw_ref[1:2, :] * z1 + cw_ref[2:3, :] * z
    a_ref[0] = (b_gate * y).astype(BF16)
    tail = z[tm - SUBLANES:tm, :]
    carry_ref[...] = tail
    st_ref[0] = tail


def conv_prompt(x3, shift, scale, w_in, conv_w8, tm=ROW_TILE):
    b, t, d = x3.shape
    mod_spec = pl.BlockSpec((1, 1, d), lambda bi, i: (bi, 0, 0))
    return pl.pallas_call(
        _conv_prompt_kernel,
        grid=(b, t // tm),
        in_specs=[pl.BlockSpec((1, tm, d), lambda bi, i: (bi, i, 0)), mod_spec, mod_spec,
                  pl.BlockSpec(w_in.shape, lambda bi, i: (0, 0), pipeline_mode=pl.Buffered(1)),
                  pl.BlockSpec(conv_w8.shape, lambda bi, i: (0, 0))],
        out_specs=[pl.BlockSpec((1, tm, d), lambda bi, i: (bi, i, 0)),
                   pl.BlockSpec((1, SUBLANES, d), lambda bi, i: (bi, 0, 0))],
        out_shape=[jax.ShapeDtypeStruct((b, t, d), BF16),
                   jax.ShapeDtypeStruct((b, SUBLANES, d), F32)],
        scratch_shapes=[pltpu.VMEM((SUBLANES, d), F32)],
        compiler_params=_params("arbitrary", "arbitrary"),
        name="conv_prompt",
    )(x3, shift, scale, w_in, conv_w8)


def _conv_sample_kernel(x_ref, sh_ref, sc_ref, win_ref, cw_ref, p0_ref, p1_ref, a_ref, z_ref):
    d = x_ref.shape[-1]
    hb = _modulate(x_ref[...], sh_ref[0], sc_ref[0]).astype(BF16)
    b_gate = _dot(hb, win_ref[:, 0:d])
    z = _dot(hb, win_ref[:, d:2 * d]) * _dot(hb, win_ref[:, 2 * d:3 * d])
    y = cw_ref[0:1, :] * p0_ref[...] + cw_ref[1:2, :] * p1_ref[...] + cw_ref[2:3, :] * z
    a_ref[...] = (b_gate * y).astype(BF16)
    z_ref[...] = z


def conv_sample(x, shift, scale, w_in, conv_w8, prefix0, prefix1):
    n, d = x.shape
    full = pl.BlockSpec((n, d), lambda i: (0, 0))
    mod_spec = pl.BlockSpec((1, n, d), lambda i: (0, 0, 0))
    return pl.pallas_call(
        _conv_sample_kernel,
        grid=(1,),
        in_specs=[full, mod_spec, mod_spec,
                  pl.BlockSpec(w_in.shape, lambda i: (0, 0), pipeline_mode=pl.Buffered(1)),
                  pl.BlockSpec(conv_w8.shape, lambda i: (0, 0)), full, full],
        out_specs=[full, full],
        out_shape=[jax.ShapeDtypeStruct((n, d), BF16), jax.ShapeDtypeStruct((n, d), F32)],
        compiler_params=_params("arbitrary"),
        name="conv_sample",
    )(x, shift, scale, w_in, conv_w8, prefix0, prefix1)


def kernel(x_prompt, x_sample, cache_k, cache_v, cache_kidx, state_conv, page_table, c_prompt, c_sample, w_ada, b_ada, w_ffn_in, w_ffn_out, w_attn_in, w_attn_out, q_norm_gain, k_norm_gain, w_conv_in, conv_w, w_conv_out):
    b, t, d = x_prompt.shape
    db, dt, _ = x_sample.shape
    assert dt == 1, "the sample path handles one new token per sequence"
    depth = w_ada.shape[0]
    head_dim = q_norm_gain.shape[-1]
    n_heads = w_attn_out.shape[1] // head_dim
    kvh = cache_k.shape[3]
    idx_dim = cache_kidx.shape[-1]
    hd, kvd = n_heads * head_dim, kvh * head_dim
    idx_heads = (w_attn_in.shape[-1] - hd - 2 * kvd - idx_dim) // (idx_dim + 1)
    ihd = idx_heads * idx_dim
    conv_width = conv_w.shape[1]
    assert conv_width == 3
    dims = dict(n_heads=n_heads, group=n_heads // kvh, head_dim=head_dim, hd=hd, kvd=kvd, ihd=ihd,
                idx_dim=idx_dim, idx_heads=idx_heads, v_rows=head_dim + BF16_SUBLANES,
                qk_scale=head_dim ** -0.5,
                wi_scale=idx_heads ** -0.5 * idx_dim ** -0.5)

    n_c = b + db
    n_c_pad = -(-n_c // SUBLANES) * SUBLANES
    c_all = jnp.concatenate([c_prompt, c_sample, jnp.zeros((n_c_pad - n_c, d), F32)], axis=0)
    mods = ada_mods(c_all, w_ada, b_ada).reshape(depth, n_c_pad, N_MOD, d)

    def prompt_mod(layer, j):
        return mods[layer, :b, j][:, None, :]

    def sample_mod(layer, j):
        return mods[layer, b:n_c, j][None]

    gw = 256 if kvd % 256 == 0 and hd % 256 == 0 else kvd
    gi = jnp.arange(gw) // head_dim
    g_mat = jnp.where(gi[:, None] == gi[None, :], 1.0 / head_dim, 0.0).astype(BF16)

    xp = x_prompt.reshape(b * t, d)
    xs = x_sample.reshape(db, d)
    pk, pv, pki, pconv, sk, sv, ski, sconv = [], [], [], [], [], [], [], []
    n_mixers = 2
    for layer in range(depth):
        li = layer // n_mixers
        w1_in, w1_out = w_ffn_in[layer, 0].astype(BF16), w_ffn_out[layer, 0].astype(BF16)
        w2_in, w2_out = w_ffn_in[layer, 1].astype(BF16), w_ffn_out[layer, 1].astype(BF16)
        pm = [prompt_mod(layer, j) for j in range(N_MOD)]
        sm = [sample_mod(layer, j) for j in range(N_MOD)]
        xp = ffn_call(xp, pm[0], pm[1], pm[2], w1_in, w1_out)
        xs = ffn_call(xs, sm[0], sm[1], sm[2], w1_in, w1_out)
        if layer % n_mixers == 0:
            wa = w_attn_in[li].astype(BF16)
            o1, o2, o3, o4, o5 = hd, hd + kvd, hd + 2 * kvd, hd + 2 * kvd + ihd, hd + 2 * kvd + ihd + idx_dim
            wkw = jnp.pad(wa[:, o4:], ((0, 0), (0, 128 - (idx_dim + idx_heads))))
            aw = dict(wq=wa[:, :o1], wk=wa[:, o1:o2], wv=wa[:, o2:o3], wqi=wa[:, o3:o4], wkw=wkw,
                      gq=jnp.tile(q_norm_gain[li], n_heads)[None, :],
                      gk=jnp.tile(k_norm_gain[li], kvh)[None, :], g=g_mat)
            w_mix_out = w_attn_out[li].astype(BF16)
            qt, k, kb, v, vt, qit, ki, kib, wit = attn_proj_prompt(xp.reshape(b, t, d), pm[3], pm[4], aw, dims)
            ap = dsa_prompt(qt, qit, wit, kb, vt, kib, dims).reshape(b * t, hd)
            pk.append(k.reshape(b, t, kvh, head_dim))
            pv.append(v.reshape(b, t, kvh, head_dim))
            pki.append(ki)
            qs, ks, vs, qis, kis, wis = attn_proj_sample(xs, sm[3], sm[4], aw, dims)
            a_s = dsa_sample(qs, ks, vs, qis, kis, wis, cache_k, cache_v, cache_kidx, li,
                             page_table, dims)
            sk.append(ks.reshape(db, 1, kvh, head_dim))
            sv.append(vs.reshape(db, 1, kvh, head_dim))
            ski.append(kis.reshape(db, 1, idx_dim))
        else:
            wc_in = w_conv_in[li].astype(BF16)
            w_mix_out = w_conv_out[li].astype(BF16)
            cw8 = jnp.pad(conv_w[li], ((0, SUBLANES - conv_width), (0, 0)))
            ap, st = conv_prompt(xp.reshape(b, t, d), pm[3], pm[4], wc_in, cw8)
            ap = ap.reshape(b * t, d)
            pconv.append(st[:, SUBLANES - (conv_width - 1):, :])
            prefix = state_conv[li]
            a_s, zs = conv_sample(xs, sm[3], sm[4], wc_in, cw8, prefix[:, 0], prefix[:, 1])
            sconv.append(jnp.stack([prefix[:, 1], zs], axis=1))
        xp = ffn_call(xp, pm[6], pm[7], pm[8], w2_in, w2_out, pre=(ap, w_mix_out, pm[5]))
        xs = ffn_call(xs, sm[6], sm[7], sm[8], w2_in, w2_out, pre=(a_s, w_mix_out, sm[5]))

    return (xp.reshape(b, t, d), xs.reshape(db, 1, d),
            jnp.stack(pk), jnp.stack(pv), jnp.stack(pki), jnp.stack(pconv),
            jnp.stack(sk), jnp.stack(sv), jnp.stack(ski), jnp.stack(sconv))
```

```python
import functools

import jax
import jax.numpy as jnp
from jax import lax
from jax.experimental import pallas as pl
from jax.experimental.pallas import tpu as pltpu

F32 = jnp.float32
BF16 = jnp.bfloat16
I32 = jnp.int32
I16 = jnp.int16

RMS_EPS = 1e-6
TOPK_MAX = 256
N_MOD = 9
SUBLANES = 8
BF16_SUBLANES = 16
LOG2_E = 1.4426950408889634
NEG_INF = float("-inf")
INT_MIN = -(2 ** 31)
INT16_MIN = -(2 ** 15)
HALF16 = 2 ** 15
KEY_NEG_INF = (0xFF800000 ^ 0x7FFFFFFF) - (1 << 32)

VMEM_LIMIT_BYTES = 56 * 1024 * 1024
ROW_TILE = 256
FFN_ROW_TILE = 512
FF_CHUNK = 256
SAMPLE_KEY_CHUNK = 2048


def _params(*sem):
    return pltpu.CompilerParams(dimension_semantics=sem, vmem_limit_bytes=VMEM_LIMIT_BYTES)


def _sigmoid(x):
    return 1.0 / (1.0 + jnp.exp(-x))


def _modulate(x, shift, scale):
    ms = jnp.mean(x * x, axis=-1, keepdims=True)
    return (x * lax.rsqrt(ms + RMS_EPS)) * (1.0 + scale) + shift


def _dot(a, b):
    return jnp.dot(a, b, preferred_element_type=F32)


def _dot_nt(a, b):
    return lax.dot_general(a, b, (((1,), (1,)), ((), ())), preferred_element_type=F32)


def _float_key(x):
    bits = lax.bitcast_convert_type(x, I32)
    return bits ^ (lax.shift_right_arithmetic(bits, 31) & 0x7FFFFFFF)


def _fold_rows(x, op, rows=SUBLANES):
    acc = x[0:rows]
    for j in range(1, x.shape[0] // rows):
        acc = op(acc, x[j * rows:(j + 1) * rows])
    return acc


def _ada_kernel(c_ref, w_ref, b_ref, o_ref):
    c = c_ref[...]
    s = (c * _sigmoid(c)).astype(BF16)
    o_ref[0] = _dot(s, w_ref[0].astype(BF16)) + b_ref[0]


def ada_mods(c_all, w_ada, b_ada):
    depth, d, nd = w_ada.shape
    r = c_all.shape[0]
    return pl.pallas_call(
        _ada_kernel,
        grid=(depth, nd // d),
        in_specs=[
            pl.BlockSpec((r, d), lambda l, j: (0, 0)),
            pl.BlockSpec((1, d, d), lambda l, j: (l, 0, j)),
            pl.BlockSpec((1, 1, d), lambda l, j: (l, 0, j)),
        ],
        out_specs=pl.BlockSpec((1, r, d), lambda l, j: (l, 0, j)),
        out_shape=jax.ShapeDtypeStruct((depth, r, nd), F32),
        compiler_params=_params("arbitrary", "arbitrary"),
        name="ada_mods",
    )(c_all, w_ada, b_ada.reshape(depth, 1, nd))


def _ffn_kernel(*refs, has_pre, d_ff, tf):
    if has_pre:
        (x_ref, a_ref, wpre_ref, gpre_ref, sh_ref, sc_ref, g_ref, win_ref, wout_ref,
         o_ref, acc_ref) = refs
    else:
        x_ref, sh_ref, sc_ref, g_ref, win_ref, wout_ref, o_ref, acc_ref = refs
    x = x_ref[...]
    if has_pre:
        x = x + gpre_ref[0] * _dot(a_ref[...], wpre_ref[...])
    hb = _modulate(x, sh_ref[0], sc_ref[0]).astype(BF16)
    for c in range(d_ff // tf):
        u = _dot(hb, win_ref[:, c * tf:(c + 1) * tf])
        g = _dot(hb, win_ref[:, d_ff + c * tf:d_ff + (c + 1) * tf])
        act = ((u * _sigmoid(u)) * g).astype(BF16)
        y = _dot(act, wout_ref[c * tf:(c + 1) * tf, :])
        if c == 0:
            acc_ref[...] = y
        else:
            acc_ref[...] += y
    o_ref[...] = x + 0.5 * g_ref[0] * acc_ref[...]


def ffn_call(x, shift, scale, gate, w_in, w_out, pre=None, tm=FFN_ROW_TILE):
    n, d = x.shape
    tm = min(tm, n)
    groups = shift.shape[0]
    tiles_per_group = n // tm // groups
    d_ff = w_out.shape[0]
    tf = FF_CHUNK if d_ff % FF_CHUNK == 0 else d_ff

    def row_spec(width):
        return pl.BlockSpec((tm, width), lambda i: (i, 0))

    def mod_spec(m):
        return pl.BlockSpec((1,) + m.shape[1:], lambda i: (i // tiles_per_group, 0, 0))

    def whole(w):
        return pl.BlockSpec(w.shape, lambda i: (0, 0), pipeline_mode=pl.Buffered(1))

    args, specs = [x], [row_spec(d)]
    if pre is not None:
        a, w_pre, g_pre = pre
        args += [a, w_pre, g_pre]
        specs += [row_spec(a.shape[1]), whole(w_pre), mod_spec(g_pre)]
    args += [shift, scale, gate, w_in, w_out]
    specs += [mod_spec(shift), mod_spec(scale), mod_spec(gate), whole(w_in), whole(w_out)]
    return pl.pallas_call(
        functools.partial(_ffn_kernel, has_pre=pre is not None, d_ff=d_ff, tf=tf),
        grid=(n // tm,),
        in_specs=specs,
        out_specs=row_spec(d),
        out_shape=jax.ShapeDtypeStruct((n, d), F32),
        scratch_shapes=[pltpu.VMEM((tm, d), F32)],
        compiler_params=_params("arbitrary"),
        name="ffn_pre" if pre is not None else "ffn",
    )(*args)


def _head_mean_square(y, g_ref):
    y2 = (y * y).astype(BF16)
    w = g_ref.shape[0]
    parts = [_dot(y2[:, j * w:(j + 1) * w], g_ref[...]) for j in range(y.shape[1] // w)]
    return parts[0] if len(parts) == 1 else jnp.concatenate(parts, axis=1)


def _project(x, sh, sc, wq_ref, wk_ref, wv_ref, wqi_ref, wkw_ref, gq_ref, gk_ref, g_ref,
             qk_scale):
    hb = _modulate(x, sh, sc).astype(BF16)
    q = _dot(hb, wq_ref[...])
    qn = q * lax.rsqrt(_head_mean_square(q, g_ref) + RMS_EPS) * gq_ref[...] * qk_scale
    k = _dot(hb, wk_ref[...])
    kn = k * lax.rsqrt(_head_mean_square(k, g_ref) + RMS_EPS) * gk_ref[...]
    v = _dot(hb, wv_ref[...])
    qi = _dot(hb, wqi_ref[...])
    kw = _dot(hb, wkw_ref[...])
    return qn, kn, v, qi, kw


def _proj_kernel(x_ref, sh_ref, sc_ref, wq_ref, wk_ref, wv_ref, wqi_ref, wkw_ref, gq_ref,
                 gk_ref, g_ref,
                 qt_ref, k_ref, kb_ref, v_ref, vt_ref, qit_ref, ki_ref, kib_ref, wit_ref,
                 *, idx_dim, idx_heads, head_dim, v_rows, qk_scale, wi_scale):
    qn, kn, v, qi, kw = _project(x_ref[0], sh_ref[0], sc_ref[0], wq_ref, wk_ref, wv_ref,
                                 wqi_ref, wkw_ref, gq_ref, gk_ref, g_ref, qk_scale)
    tm = x_ref.shape[1]
    qt_ref[0, 0] = qn.T.astype(BF16)
    k_ref[0] = kn
    kb_ref[0] = kn.astype(BF16)
    v_ref[0] = v
    vt = v.T.astype(BF16)
    ones_rows = (lax.broadcasted_iota(I32, (v_rows - head_dim, tm), 0) == 0).astype(BF16)
    for g in range(v.shape[1] // head_dim):
        vt_ref[0, 0, g * v_rows:g * v_rows + head_dim, :] = vt[g * head_dim:(g + 1) * head_dim]
        vt_ref[0, 0, g * v_rows + head_dim:(g + 1) * v_rows, :] = ones_rows
    qit_ref[0, 0] = qi.T.astype(BF16)
    ki = kw[:, :idx_dim]
    ki_ref[0] = ki
    kib_ref[0] = ki.astype(BF16)
    wit_ref[0, 0] = kw.T[idx_dim:idx_dim + idx_heads, :] * wi_scale


def _sample_proj_kernel(x_ref, sh_ref, sc_ref, wq_ref, wk_ref, wv_ref, wqi_ref, wkw_ref,
                        gq_ref, gk_ref, g_ref,
                        q_ref, k_ref, v_ref, qi_ref, ki_ref, wi_ref,
                        *, idx_dim, idx_heads, qk_scale, wi_scale):
    qn, kn, v, qi, kw = _project(x_ref[...], sh_ref[0], sc_ref[0], wq_ref, wk_ref, wv_ref,
                                 wqi_ref, wkw_ref, gq_ref, gk_ref, g_ref, qk_scale)
    q_ref[...] = qn.astype(BF16)
    k_ref[...] = kn
    v_ref[...] = v
    qi_ref[...] = qi.astype(BF16)
    ki_ref[...] = kw[:, :idx_dim]
    wi_ref[...] = kw[:, idx_dim:idx_dim + idx_heads] * wi_scale


def _proj_weight_specs(ws, n_grid):
    zeros = (lambda *idx: (0, 0))
    return [pl.BlockSpec(w.shape, zeros, pipeline_mode=pl.Buffered(1)) for w in ws]


def attn_proj_prompt(x3, shift, scale, aw, dims, tm=ROW_TILE):
    b, t, d = x3.shape
    nt = t // tm
    hd, kvd, ihd, di, ih = dims["hd"], dims["kvd"], dims["ihd"], dims["idx_dim"], dims["idx_heads"]
    ws = [aw["wq"], aw["wk"], aw["wv"], aw["wqi"], aw["wkw"], aw["gq"], aw["gk"], aw["g"]]
    mod_spec = pl.BlockSpec((1, 1, d), lambda bi, i: (bi, 0, 0))
    nat = lambda w: pl.BlockSpec((1, tm, w), lambda bi, i: (bi, i, 0))
    tr = lambda r: pl.BlockSpec((1, 1, r, tm), lambda bi, i: (bi, i, 0, 0))
    vtd = kvd // dims["head_dim"] * dims["v_rows"]
    return pl.pallas_call(
        functools.partial(_proj_kernel, idx_dim=di, idx_heads=ih, head_dim=dims["head_dim"],
                          v_rows=dims["v_rows"], qk_scale=dims["qk_scale"] * LOG2_E,
                          wi_scale=dims["wi_scale"]),
        grid=(b, nt),
        in_specs=[nat(d), mod_spec, mod_spec] + _proj_weight_specs(ws, 2),
        out_specs=[tr(hd), nat(kvd), nat(kvd), nat(kvd), tr(vtd), tr(ihd), nat(di), nat(di), tr(ih)],
        out_shape=[
            jax.ShapeDtypeStruct((b, nt, hd, tm), BF16),
            jax.ShapeDtypeStruct((b, t, kvd), F32),
            jax.ShapeDtypeStruct((b, t, kvd), BF16),
            jax.ShapeDtypeStruct((b, t, kvd), F32),
            jax.ShapeDtypeStruct((b, nt, vtd, tm), BF16),
            jax.ShapeDtypeStruct((b, nt, ihd, tm), BF16),
            jax.ShapeDtypeStruct((b, t, di), F32),
            jax.ShapeDtypeStruct((b, t, di), BF16),
            jax.ShapeDtypeStruct((b, nt, ih, tm), F32),
        ],
        compiler_params=_params("arbitrary", "arbitrary"),
        name="attn_proj",
    )(x3, shift, scale, *ws)


def attn_proj_sample(x, shift, scale, aw, dims):
    n, d = x.shape
    hd, kvd, ihd, di, ih = dims["hd"], dims["kvd"], dims["ihd"], dims["idx_dim"], dims["idx_heads"]
    ws = [aw["wq"], aw["wk"], aw["wv"], aw["wqi"], aw["wkw"], aw["gq"], aw["gk"], aw["g"]]
    full = lambda w: pl.BlockSpec((n, w), lambda i: (0, 0))
    mod_spec = pl.BlockSpec((1, n, d), lambda i: (0, 0, 0))
    return pl.pallas_call(
        functools.partial(_sample_proj_kernel, idx_dim=di, idx_heads=ih,
                          qk_scale=dims["qk_scale"], wi_scale=dims["wi_scale"]),
        grid=(1,),
        in_specs=[full(d), mod_spec, mod_spec] + _proj_weight_specs(ws, 1),
        out_specs=[full(hd), full(kvd), full(kvd), full(ihd), full(di), full(ih)],
        out_shape=[
            jax.ShapeDtypeStruct((n, hd), BF16),
            jax.ShapeDtypeStruct((n, kvd), F32),
            jax.ShapeDtypeStruct((n, kvd), F32),
            jax.ShapeDtypeStruct((n, ihd), BF16),
            jax.ShapeDtypeStruct((n, di), F32),
            jax.ShapeDtypeStruct((n, ih), F32),
        ],
        compiler_params=_params("arbitrary"),
        name="attn_proj_sample",
    )(x, shift, scale, *ws)


def _dsa_kernel(qt_ref, qit_ref, wit_ref, k_ref, vt_ref, ki_ref, o_ref,
                keys_ref, hi_ref, lo_ref, qpad_ref, s_ref, acc_ref, m_ref, mnew_ref, l_ref, jlim_ref,
                *, n_heads, group, head_dim, v_rows, idx_heads, idx_dim, topk, seq_bits):
    tq = qt_ref.shape[-1]
    tk = tq
    tk2 = 2 * tk
    i = pl.program_id(1)
    n_chunks = i + 1
    n_pairs = (n_chunks + 1) // 2
    q_idx = i * tq + lax.broadcasted_iota(I32, (tk, tq), 1)
    row_iota = lax.broadcasted_iota(I32, (tk, tq), 0)
    q_idx2 = i * tq + lax.broadcasted_iota(I32, (tk2, tq), 1)
    row_iota2 = lax.broadcasted_iota(I32, (tk2, tq), 0)

    def chunk_rows(c):
        return pl.ds(pl.multiple_of(c * tk, tk), tk)

    def pair_rows(j):
        return pl.ds(pl.multiple_of(j * tk2, tk2), tk2)

    def score_chunk(c, carry):
        kic = ki_ref[0, chunk_rows(c), :]
        acc = jnp.zeros((tk, tq), F32)
        for h in range(idx_heads):
            s = _dot(kic, qit_ref[0, 0, h * idx_dim:(h + 1) * idx_dim, :])
            acc = acc + jnp.maximum(s, 0.0) * wit_ref[0, 0, h:h + 1, :]
        acc = jnp.where(c * tk + row_iota <= q_idx, acc, NEG_INF)
        key = _float_key(acc)
        keys_ref[chunk_rows(c), :] = key
        hi_ref[chunk_rows(c), :] = lax.shift_right_arithmetic(key, 16).astype(I16)
        lo_ref[chunk_rows(c), :] = ((key & 0xFFFF) - HALF16).astype(I16)
        return carry

    lax.fori_loop(0, n_chunks, score_chunk, 0)

    @pl.when(n_chunks % 2 == 1)
    def _():
        keys_ref[chunk_rows(n_chunks), :] = jnp.full((tk, tq), INT_MIN, I32)
        hi_ref[chunk_rows(n_chunks), :] = jnp.full((tk, tq), INT16_MIN, I16)
        lo_ref[chunk_rows(n_chunks), :] = jnp.full((tk, tq), INT16_MIN, I16)

    def count(pred):
        def body(j, cnt):
            kidx = j * tk2 + row_iota2
            return cnt + _fold_rows(pred(keys_ref[pair_rows(j), :], kidx).astype(I32), jnp.add)
        cnt8 = lax.fori_loop(0, n_pairs, body, jnp.zeros((SUBLANES, tq), I32))
        return jnp.sum(cnt8, axis=0, keepdims=True)

    def count_ge16(ref, cand):
        cand16 = cand.astype(I16)
        def body(j, cnt):
            m = (ref[pair_rows(j), :] >= cand16).astype(I16)
            return cnt + _fold_rows(m, jnp.add, BF16_SUBLANES).astype(I32)
        cnt16 = lax.fori_loop(0, n_pairs, body, jnp.zeros((BF16_SUBLANES, tq), I32))
        return jnp.sum(cnt16, axis=0, keepdims=True)

    def bisect16(ref, k_need, n_all):
        zero = jnp.zeros((1, tq), I32)
        c0 = count_ge16(ref, zero)
        nonneg = c0 >= k_need
        init = (jnp.where(nonneg, zero, INT16_MIN), jnp.where(nonneg, c0, n_all),
                jnp.where(nonneg, zero, c0))

        def search(it, carry):
            prefix, n_ge, n_up = carry
            cand = prefix | jnp.left_shift(jnp.int32(1), 14 - it)
            c = count_ge16(ref, cand)
            ok = c >= k_need
            return jnp.where(ok, cand, prefix), jnp.where(ok, c, n_ge), jnp.where(ok, n_up, c)

        return lax.fori_loop(0, 15, search, init)

    zero = jnp.zeros((1, tq), I32)
    thr_hi, n_ge_hi, n_gt_hi = bisect16(hi_ref, topk, n_chunks * tk)

    thr_hi16 = thr_hi.astype(I16)

    def mask_low(j, carry):
        keep = hi_ref[pair_rows(j), :] == thr_hi16
        lo_ref[pair_rows(j), :] = jnp.where(keep, lo_ref[pair_rows(j), :], INT16_MIN).astype(I16)
        return carry

    lax.fori_loop(0, n_pairs, mask_low, 0)
    thr_lo, n_ge_lo, n_gt_lo = bisect16(lo_ref, topk - n_gt_hi, n_ge_hi - n_gt_hi)
    thr = lax.shift_left(thr_hi, 16) | (thr_lo + HALF16)
    n_ge = n_gt_hi + n_ge_lo
    n_gt = n_gt_hi + n_gt_lo
    need = topk - n_gt

    jlim_ref[...] = jnp.full((1, tq), (1 << seq_bits) - 1, I32)
    excess = jnp.logical_and(n_ge > topk, thr > KEY_NEG_INF)

    @pl.when(jnp.max(excess.astype(I32)) > 0)
    def _():
        def tie_search(it, j):
            cand = j | jnp.left_shift(jnp.int32(1), seq_bits - 1 - it)
            n_before = count(lambda kc, kidx: jnp.logical_and(kc == thr, kidx < cand))
            return jnp.where(n_before < need, cand, j)
        jlim_ref[...] = lax.fori_loop(0, seq_bits, tie_search, zero)

    jlim = jlim_ref[...]

    def bias_pair(j, carry):
        kc = keys_ref[pair_rows(j), :]
        kidx = j * tk2 + row_iota2
        sel = jnp.logical_or(kc > thr, jnp.logical_and(kc == thr, kidx <= jlim))
        sel = jnp.logical_and(sel, kidx <= q_idx2)
        bias = jnp.where(sel, 0.0, NEG_INF).astype(F32)
        keys_ref[pair_rows(j), :] = lax.bitcast_convert_type(bias, I32)
        return carry

    lax.fori_loop(0, n_pairs, bias_pair, 0)

    qpad_ref[...] = jnp.zeros(qpad_ref.shape, BF16)
    for h in range(n_heads):
        g = h // group
        qpad_ref[h, g * head_dim:(g + 1) * head_dim, :] = qt_ref[0, 0, h * head_dim:(h + 1) * head_dim, :]
    m_ref[...] = jnp.full(m_ref.shape, NEG_INF, F32)
    l_ref[...] = jnp.zeros(l_ref.shape, F32)
    acc_ref[...] = jnp.zeros(acc_ref.shape, F32)

    def attend(c, carry):
        kc = k_ref[0, chunk_rows(c), :]
        bias = lax.bitcast_convert_type(keys_ref[chunk_rows(c), :], F32)
        for h in range(n_heads):
            s = _dot(kc, qpad_ref[h]) + bias
            s_ref[h] = s
            mnew_ref[h:h + 1, :] = jnp.maximum(
                m_ref[h:h + 1, :], jnp.max(_fold_rows(s, jnp.maximum), axis=0, keepdims=True))
        for h in range(n_heads):
            g = h // group
            rows = slice(h * head_dim, (h + 1) * head_dim)
            m_old = m_ref[h:h + 1, :]
            m_new = mnew_ref[h:h + 1, :]
            m_safe = jnp.where(m_new == NEG_INF, 0.0, m_new)
            p = jnp.exp2(s_ref[h] - m_safe).astype(BF16)
            alpha = jnp.exp2(m_old - m_safe)
            pv = _dot(vt_ref[0, c, g * v_rows:(g + 1) * v_rows, :], p)
            acc_ref[rows, :] = alpha * acc_ref[rows, :] + pv[0:head_dim]
            l_ref[h:h + 1, :] = alpha * l_ref[h:h + 1, :] + pv[head_dim:head_dim + 1]
            m_ref[h:h + 1, :] = m_new
        return carry

    lax.fori_loop(0, n_chunks, attend, 0)
    for h in range(n_heads):
        rows = slice(h * head_dim, (h + 1) * head_dim)
        acc_ref[rows, :] = acc_ref[rows, :] / l_ref[h:h + 1, :]
    o_ref[0] = acc_ref[...].T.astype(BF16)


def dsa_prompt(qt, qit, wit, kb, vt, kib, dims):
    b, nt, hd, tq = qt.shape
    t = nt * tq
    kvd, ihd, ih = kb.shape[-1], qit.shape[2], wit.shape[2]
    nh = dims["n_heads"]
    topk = min(TOPK_MAX, t // 4)
    blk = lambda r: pl.BlockSpec((1, 1, r, tq), lambda bi, i: (bi, i, 0, 0))
    per_batch3 = lambda w: pl.BlockSpec((1, t, w), lambda bi, i: (bi, 0, 0),
                                        pipeline_mode=pl.Buffered(1))
    return pl.pallas_call(
        functools.partial(_dsa_kernel, n_heads=nh, group=dims["group"],
                          head_dim=dims["head_dim"], v_rows=dims["v_rows"], idx_heads=ih,
                          idx_dim=dims["idx_dim"], topk=topk, seq_bits=max(1, (t - 1).bit_length())),
        grid=(b, nt),
        in_specs=[blk(hd), blk(ihd), blk(ih), per_batch3(kvd),
                  pl.BlockSpec((1, nt) + vt.shape[2:], lambda bi, i: (bi, 0, 0, 0),
                               pipeline_mode=pl.Buffered(1)),
                  per_batch3(dims["idx_dim"])],
        out_specs=pl.BlockSpec((1, tq, hd), lambda bi, i: (bi, i, 0)),
        out_shape=jax.ShapeDtypeStruct((b, t, hd), BF16),
        scratch_shapes=[pltpu.VMEM((t + tq, tq), I32), pltpu.VMEM((t + tq, tq), I16),
                        pltpu.VMEM((t + tq, tq), I16), pltpu.VMEM((nh, kvd, tq), BF16),
                        pltpu.VMEM((nh, tq, tq), F32), pltpu.VMEM((hd, tq), F32),
                        pltpu.VMEM((nh, tq), F32), pltpu.VMEM((nh, tq), F32),
                        pltpu.VMEM((nh, tq), F32), pltpu.VMEM((1, tq), I32)],
        compiler_params=_params("arbitrary", "arbitrary"),
        name="dsa_prompt",
    )(qt, qit, wit, kb, vt, kib)


def _dsa_sample_kernel(pt_ref, qpad_ref, hsel_ref, qi_ref, wi_ref, knew_ref, vnew_ref, kinew_ref,
                       ckt_ref, cvt_ref, ckit_ref, o_ref,
                       ktbuf, vtbuf, kitbuf, sem,
                       *, layer, n_pages, page, topk, total_bits, n_groups, head_dim, key_chunk):
    bi = pl.program_id(0)
    n_seq = pl.num_programs(0)
    past = n_pages * page
    slot = bi % 2

    def page_copies(seq, buf, p):
        phys = pt_ref[seq, p]
        cols = pl.ds(pl.multiple_of(p * page, page), page)
        return (pltpu.make_async_copy(ckt_ref.at[layer, phys], ktbuf.at[buf, :, cols], sem.at[buf, 0]),
                pltpu.make_async_copy(cvt_ref.at[layer, phys], vtbuf.at[buf, :, cols], sem.at[buf, 1]),
                pltpu.make_async_copy(ckit_ref.at[layer, phys], kitbuf.at[buf, :, cols], sem.at[buf, 2]))

    def start_seq(seq, buf):
        def body(p, carry):
            for cp in page_copies(seq, buf, p):
                cp.start()
            return carry
        lax.fori_loop(0, n_pages, body, 0)

    def wait_seq(seq, buf):
        def body(p, carry):
            for cp in page_copies(seq, buf, p):
                cp.wait()
            return carry
        lax.fori_loop(0, n_pages, body, 0)

    @pl.when(bi == 0)
    def _():
        start_seq(0, 0)

    @pl.when(bi + 1 < n_seq)
    def _():
        start_seq(bi + 1, 1 - slot)

    wait_seq(bi, slot)
    chunks = [slice(c * key_chunk, (c + 1) * key_chunk) for c in range(past // key_chunk)]

    qi = qi_ref[0]
    wi = wi_ref[0]
    score = jnp.concatenate(
        [jnp.sum(jnp.maximum(_dot(qi, kitbuf[slot, :, ch].astype(BF16)), 0.0) * wi, axis=0,
                 keepdims=True) for ch in chunks], axis=1) + 0.0
    s_new = jnp.sum(qi.astype(F32) * kinew_ref[0], axis=1, keepdims=True)
    score_new = jnp.sum(jnp.maximum(s_new, 0.0) * wi, axis=0, keepdims=True) + 0.0
    keys = _float_key(score)
    key_new = _float_key(score_new)
    kidx = lax.broadcasted_iota(I32, (1, past), 1)

    def count(pred):
        n = jnp.sum(pred(keys, kidx).astype(I32), axis=1, keepdims=True)
        return n + pred(key_new, jnp.full((1, 1), past, I32)).astype(I32)

    zero = jnp.zeros((1, 1), I32)
    prefix = jnp.where(count(lambda k, j: k >= zero) >= topk, zero, INT_MIN)

    def search(it, prefix):
        cand = prefix | jnp.left_shift(jnp.int32(1), 30 - it)
        return jnp.where(count(lambda k, j: k >= cand) >= topk, cand, prefix)

    thr = lax.fori_loop(0, 31, search, prefix)
    need = topk - count(lambda k, j: k > thr)

    def tie_search(it, jl):
        cand = jl | jnp.left_shift(jnp.int32(1), total_bits - 1 - it)
        n_before = count(lambda k, j: jnp.logical_and(k == thr, j < cand))
        return jnp.where(n_before < need, cand, jl)

    jlim = lax.fori_loop(0, total_bits, tie_search, zero)

    def selected(k, j):
        return jnp.logical_or(k > thr, jnp.logical_and(k == thr, j <= jlim))

    sel = selected(keys, kidx)
    sel_new = selected(key_new, jnp.full((1, 1), past, I32))

    qpad = qpad_ref[0]
    logits = jnp.concatenate([_dot(qpad, ktbuf[slot, :, ch].astype(BF16)) for ch in chunks], axis=1)
    logits = jnp.where(sel, logits, NEG_INF)
    logit_new = jnp.sum(qpad.astype(F32) * knew_ref[0], axis=1, keepdims=True)
    logit_new = jnp.where(sel_new, logit_new, NEG_INF)
    m = jnp.maximum(jnp.max(logits, axis=1, keepdims=True), logit_new)
    p = jnp.exp(logits - m)
    p_new = jnp.exp(logit_new - m)
    denom = jnp.sum(p, axis=1, keepdims=True) + p_new
    pb = p.astype(BF16)
    o_all = p_new * vnew_ref[0]
    for ch in chunks:
        o_all = o_all + _dot_nt(pb[:, ch], vtbuf[slot, :, ch].astype(BF16))
    o_all = o_all / denom
    out = jnp.zeros((o_all.shape[0], head_dim), F32)
    for g in range(n_groups):
        out = out + hsel_ref[g] * o_all[:, g * head_dim:(g + 1) * head_dim]
    o_ref[0] = out.astype(BF16)


def dsa_sample(q, k_new, v_new, qi, ki_new, wi, cache_kt, cache_vt, cache_kit, layer, page_table, dims):
    db, hd = q.shape
    n_layers, n_pool, kvd, page = cache_kt.shape
    dh = dims["head_dim"]
    kvh = kvd // dh
    n_pages = page_table.shape[1]
    past = n_pages * page
    nh, group, ih, di = dims["n_heads"], dims["group"], dims["idx_heads"], dims["idx_dim"]
    topk = min(TOPK_MAX, (past + 1) // 4)
    key_chunk = SAMPLE_KEY_CHUNK if past % SAMPLE_KEY_CHUNK == 0 else past
    head_group = jnp.arange(nh) // group
    onehot = (head_group[:, None] == jnp.arange(kvh)[None, :])
    qpad = (q.reshape(db, nh, 1, dh) * onehot[None, :, :, None].astype(q.dtype)).reshape(db, nh, kvd)
    hsel = jnp.transpose(onehot.astype(F32))[:, :, None]
    row = lambda w: pl.BlockSpec((1, 1, w), lambda bi, pt: (bi, 0, 0))
    any_spec = pl.BlockSpec(memory_space=pl.ANY)
    grid_spec = pltpu.PrefetchScalarGridSpec(
        num_scalar_prefetch=1,
        grid=(db,),
        in_specs=[
            pl.BlockSpec((1, nh, kvd), lambda bi, pt: (bi, 0, 0)),
            pl.BlockSpec((kvh, nh, 1), lambda bi, pt: (0, 0, 0)),
            pl.BlockSpec((1, ih, di), lambda bi, pt: (bi, 0, 0)),
            pl.BlockSpec((1, ih, 1), lambda bi, pt: (bi, 0, 0)),
            row(kvd), row(kvd), row(di),
            any_spec, any_spec, any_spec,
        ],
        out_specs=pl.BlockSpec((1, nh, dh), lambda bi, pt: (bi, 0, 0)),
        scratch_shapes=[pltpu.VMEM((2, kvd, past), F32), pltpu.VMEM((2, kvd, past), F32),
                        pltpu.VMEM((2, di, past), F32), pltpu.SemaphoreType.DMA((2, 3))],
    )
    out = pl.pallas_call(
        functools.partial(_dsa_sample_kernel, layer=layer, n_pages=n_pages, page=page, topk=topk,
                          total_bits=max(1, past.bit_length()), n_groups=kvh, head_dim=dh,
                          key_chunk=key_chunk),
        grid_spec=grid_spec,
        out_shape=jax.ShapeDtypeStruct((db, nh, dh), BF16),
        compiler_params=_params("arbitrary"),
        name="dsa_sample",
    )(page_table, qpad, hsel, qi.reshape(db, ih, di), wi.reshape(db, ih, 1),
      k_new.reshape(db, 1, kvd), v_new.reshape(db, 1, kvd), ki_new.reshape(db, 1, di),
      cache_kt, cache_vt, cache_kit)
    return out.reshape(db, hd)


def _conv_prompt_kernel(x_ref, sh_ref, sc_ref, win_ref, cw_ref, a_ref, st_ref, carry_ref):
    i = pl.program_id(1)
    d = x_ref.shape[-1]
    tm = x_ref.shape[1]
    hb = _modulate(x_ref[0], sh_ref[0], sc_ref[0]).astype(BF16)
    b_gate = _dot(hb, win_ref[:, 0:d])
    z = _dot(hb, win_ref[:, d:2 * d]) * _dot(hb, win_ref[:, 2 * d:3 * d])

    @pl.when(i == 0)
    def _():
        carry_ref[...] = jnp.zeros_like(carry_ref)

    row = lax.broadcasted_iota(I32, (tm, d), 0)
    prev1 = carry_ref[SUBLANES - 1:SUBLANES, :]
    prev2 = carry_ref[SUBLANES - 2:SUBLANES - 1, :]
    z1 = jnp.where(row == 0, prev1, pltpu.roll(z, 1, axis=0))
    z2 = jnp.where(row == 0, prev2, jnp.where(row == 1, prev1, pltpu.roll(z, 2, axis=0)))
    y = cw_ref[0:1, :] * z2 + cw_ref[1:2, :] * z1 + cw_ref[2:3, :] * z
    a_ref[0] = (b_gate * y).astype(BF16)
    tail = z[tm - SUBLANES:tm, :]
    carry_ref[...] = tail
    st_ref[0] = tail


def conv_prompt(x3, shift, scale, w_in, conv_w8, tm=ROW_TILE):
    b, t, d = x3.shape
    mod_spec = pl.BlockSpec((1, 1, d), lambda bi, i: (bi, 0, 0))
    return pl.pallas_call(
        _conv_prompt_kernel,
        grid=(b, t // tm),
        in_specs=[pl.BlockSpec((1, tm, d), lambda bi, i: (bi, i, 0)), mod_spec, mod_spec,
                  pl.BlockSpec(w_in.shape, lambda bi, i: (0, 0), pipeline_mode=pl.Buffered(1)),
                  pl.BlockSpec(conv_w8.shape, lambda bi, i: (0, 0))],
        out_specs=[pl.BlockSpec((1, tm, d), lambda bi, i: (bi, i, 0)),
                   pl.BlockSpec((1, SUBLANES, d), lambda bi, i: (bi, 0, 0))],
        out_shape=[jax.ShapeDtypeStruct((b, t, d), BF16),
                   jax.ShapeDtypeStruct((b, SUBLANES, d), F32)],
        scratch_shapes=[pltpu.VMEM((SUBLANES, d), F32)],
        compiler_params=_params("arbitrary", "arbitrary"),
        name="conv_prompt",
    )(x3, shift, scale, w_in, conv_w8)


def _conv_sample_kernel(x_ref, sh_ref, sc_ref, win_ref, cw_ref, p0_ref, p1_ref, a_ref, z_ref):
    d = x_ref.shape[-1]
    hb = _modulate(x_ref[...], sh_ref[0], sc_ref[0]).astype(BF16)
    b_gate = _dot(hb, win_ref[:, 0:d])
    z = _dot(hb, win_ref[:, d:2 * d]) * _dot(hb, win_ref[:, 2 * d:3 * d])
    y = cw_ref[0:1, :] * p0_ref[...] + cw_ref[1:2, :] * p1_ref[...] + cw_ref[2:3, :] * z
    a_ref[...] = (b_gate * y).astype(BF16)
    z_ref[...] = z


def conv_sample(x, shift, scale, w_in, conv_w8, prefix0, prefix1):
    n, d = x.shape
    full = pl.BlockSpec((n, d), lambda i: (0, 0))
    mod_spec = pl.BlockSpec((1, n, d), lambda i: (0, 0, 0))
    return pl.pallas_call(
        _conv_sample_kernel,
        grid=(1,),
        in_specs=[full, mod_spec, mod_spec,
                  pl.BlockSpec(w_in.shape, lambda i: (0, 0), pipeline_mode=pl.Buffered(1)),
                  pl.BlockSpec(conv_w8.shape, lambda i: (0, 0)), full, full],
        out_specs=[full, full],
        out_shape=[jax.ShapeDtypeStruct((n, d), BF16), jax.ShapeDtypeStruct((n, d), F32)],
        compiler_params=_params("arbitrary"),
        name="conv_sample",
    )(x, shift, scale, w_in, conv_w8, prefix0, prefix1)


def kernel(x_prompt, x_sample, cache_k, cache_v, cache_kidx, state_conv, page_table, c_prompt, c_sample, w_ada, b_ada, w_ffn_in, w_ffn_out, w_attn_in, w_attn_out, q_norm_gain, k_norm_gain, w_conv_in, conv_w, w_conv_out):
    b, t, d = x_prompt.shape
    db, dt, _ = x_sample.shape
    assert dt == 1, "the sample path handles one new token per sequence"
    depth = w_ada.shape[0]
    head_dim = q_norm_gain.shape[-1]
    n_heads = w_attn_out.shape[1] // head_dim
    kvh = cache_k.shape[3]
    idx_dim = cache_kidx.shape[-1]
    hd, kvd = n_heads * head_dim, kvh * head_dim
    idx_heads = (w_attn_in.shape[-1] - hd - 2 * kvd - idx_dim) // (idx_dim + 1)
    ihd = idx_heads * idx_dim
    conv_width = conv_w.shape[1]
    assert conv_width == 3
    dims = dict(n_heads=n_heads, group=n_heads // kvh, head_dim=head_dim, hd=hd, kvd=kvd, ihd=ihd,
                idx_dim=idx_dim, idx_heads=idx_heads, v_rows=head_dim + BF16_SUBLANES,
                qk_scale=head_dim ** -0.5,
                wi_scale=idx_heads ** -0.5 * idx_dim ** -0.5)

    n_c = b + db
    n_c_pad = -(-n_c // SUBLANES) * SUBLANES
    c_all = jnp.concatenate([c_prompt, c_sample, jnp.zeros((n_c_pad - n_c, d), F32)], axis=0)
    mods = ada_mods(c_all, w_ada, b_ada).reshape(depth, n_c_pad, N_MOD, d)

    def prompt_mod(layer, j):
        return mods[layer, :b, j][:, None, :]

    def sample_mod(layer, j):
        return mods[layer, b:n_c, j][None]

    gw = 256 if kvd % 256 == 0 and hd % 256 == 0 else kvd
    gi = jnp.arange(gw) // head_dim
    g_mat = jnp.where(gi[:, None] == gi[None, :], 1.0 / head_dim, 0.0).astype(BF16)

    n_attn, n_pool, page = cache_k.shape[:3]
    cache_kt = jnp.transpose(cache_k, (0, 1, 3, 4, 2)).reshape(n_attn, n_pool, kvd, page)
    cache_vt = jnp.transpose(cache_v, (0, 1, 3, 4, 2)).reshape(n_attn, n_pool, kvd, page)
    cache_kit = jnp.transpose(cache_kidx, (0, 1, 3, 2))

    xp = x_prompt.reshape(b * t, d)
    xs = x_sample.reshape(db, d)
    pk, pv, pki, pconv, sk, sv, ski, sconv = [], [], [], [], [], [], [], []
    n_mixers = 2
    for layer in range(depth):
        li = layer // n_mixers
        w1_in, w1_out = w_ffn_in[layer, 0].astype(BF16), w_ffn_out[layer, 0].astype(BF16)
        w2_in, w2_out = w_ffn_in[layer, 1].astype(BF16), w_ffn_out[layer, 1].astype(BF16)
        pm = [prompt_mod(layer, j) for j in range(N_MOD)]
        sm = [sample_mod(layer, j) for j in range(N_MOD)]
        xp = ffn_call(xp, pm[0], pm[1], pm[2], w1_in, w1_out)
        xs = ffn_call(xs, sm[0], sm[1], sm[2], w1_in, w1_out)
        if layer % n_mixers == 0:
            wa = w_attn_in[li].astype(BF16)
            o1, o2, o3, o4, o5 = hd, hd + kvd, hd + 2 * kvd, hd + 2 * kvd + ihd, hd + 2 * kvd + ihd + idx_dim
            wkw = jnp.pad(wa[:, o4:], ((0, 0), (0, 128 - (idx_dim + idx_heads))))
            aw = dict(wq=wa[:, :o1], wk=wa[:, o1:o2], wv=wa[:, o2:o3], wqi=wa[:, o3:o4], wkw=wkw,
                      gq=jnp.tile(q_norm_gain[li], n_heads)[None, :],
                      gk=jnp.tile(k_norm_gain[li], kvh)[None, :], g=g_mat)
            w_mix_out = w_attn_out[li].astype(BF16)
            qt, k, kb, v, vt, qit, ki, kib, wit = attn_proj_prompt(xp.reshape(b, t, d), pm[3], pm[4], aw, dims)
            ap = dsa_prompt(qt, qit, wit, kb, vt, kib, dims).reshape(b * t, hd)
            pk.append(k.reshape(b, t, kvh, head_dim))
            pv.append(v.reshape(b, t, kvh, head_dim))
            pki.append(ki)
            qs, ks, vs, qis, kis, wis = attn_proj_sample(xs, sm[3], sm[4], aw, dims)
            a_s = dsa_sample(qs, ks, vs, qis, kis, wis, cache_kt, cache_vt, cache_kit, li,
                             page_table, dims)
            sk.append(ks.reshape(db, 1, kvh, head_dim))
            sv.append(vs.reshape(db, 1, kvh, head_dim))
            ski.append(kis.reshape(db, 1, idx_dim))
        else:
            wc_in = w_conv_in[li].astype(BF16)
            w_mix_out = w_conv_out[li].astype(BF16)
            cw8 = jnp.pad(conv_w[li], ((0, SUBLANES - conv_width), (0, 0)))
            ap, st = conv_prompt(xp.reshape(b, t, d), pm[3], pm[4], wc_in, cw8)
            ap = ap.reshape(b * t, d)
            pconv.append(st[:, SUBLANES - (conv_width - 1):, :])
            prefix = state_conv[li]
            a_s, zs = conv_sample(xs, sm[3], sm[4], wc_in, cw8, prefix[:, 0], prefix[:, 1])
            sconv.append(jnp.stack([prefix[:, 1], zs], axis=1))
        xp = ffn_call(xp, pm[6], pm[7], pm[8], w2_in, w2_out, pre=(ap, w_mix_out, pm[5]))
        xs = ffn_call(xs, sm[6], sm[7], sm[8], w2_in, w2_out, pre=(a_s, w_mix_out, sm[5]))

    return (xp.reshape(b, t, d), xs.reshape(db, 1, d),
            jnp.stack(pk), jnp.stack(pv), jnp.stack(pki), jnp.stack(pconv),
            jnp.stack(sk), jnp.stack(sv), jnp.stack(ski), jnp.stack(sconv))
```

```python
import functools

import jax
import jax.numpy as jnp
from jax import lax
from jax.experimental import pallas as pl
from jax.experimental.pallas import tpu as pltpu

F32 = jnp.float32
BF16 = jnp.bfloat16
I32 = jnp.int32
I16 = jnp.int16

RMS_EPS = 1e-6
TOPK_MAX = 256
N_MOD = 9
SUBLANES = 8
BF16_SUBLANES = 16
LOG2_E = 1.4426950408889634
NEG_INF = float("-inf")
INT_MIN = -(2 ** 31)
INT16_MIN = -(2 ** 15)
HALF16 = 2 ** 15
KEY_NEG_INF = (0xFF800000 ^ 0x7FFFFFFF) - (1 << 32)

VMEM_LIMIT_BYTES = 56 * 1024 * 1024
ROW_TILE = 256
FFN_ROW_TILE = 512
FF_CHUNK = 256
SAMPLE_KEY_CHUNK = 2048


def _params(*sem):
    return pltpu.CompilerParams(dimension_semantics=sem, vmem_limit_bytes=VMEM_LIMIT_BYTES)


def _sigmoid(x):
    return 1.0 / (1.0 + jnp.exp(-x))


def _modulate(x, shift, scale):
    ms = jnp.mean(x * x, axis=-1, keepdims=True)
    return (x * lax.rsqrt(ms + RMS_EPS)) * (1.0 + scale) + shift


def _dot(a, b):
    return jnp.dot(a, b, preferred_element_type=F32)


def _dot_nt(a, b):
    return lax.dot_general(a, b, (((1,), (1,)), ((), ())), preferred_element_type=F32)


def _float_key(x):
    bits = lax.bitcast_convert_type(x, I32)
    return bits ^ (lax.shift_right_arithmetic(bits, 31) & 0x7FFFFFFF)


def _fold_rows(x, op, rows=SUBLANES):
    acc = x[0:rows]
    for j in range(1, x.shape[0] // rows):
        acc = op(acc, x[j * rows:(j + 1) * rows])
    return acc


def _ada_kernel(c_ref, w_ref, b_ref, o_ref):
    c = c_ref[...]
    s = (c * _sigmoid(c)).astype(BF16)
    o_ref[0] = _dot(s, w_ref[0].astype(BF16)) + b_ref[0]


def ada_mods(c_all, w_ada, b_ada):
    depth, d, nd = w_ada.shape
    r = c_all.shape[0]
    return pl.pallas_call(
        _ada_kernel,
        grid=(depth, nd // d),
        in_specs=[
            pl.BlockSpec((r, d), lambda l, j: (0, 0)),
            pl.BlockSpec((1, d, d), lambda l, j: (l, 0, j)),
            pl.BlockSpec((1, 1, d), lambda l, j: (l, 0, j)),
        ],
        out_specs=pl.BlockSpec((1, r, d), lambda l, j: (l, 0, j)),
        out_shape=jax.ShapeDtypeStruct((depth, r, nd), F32),
        compiler_params=_params("arbitrary", "arbitrary"),
        name="ada_mods",
    )(c_all, w_ada, b_ada.reshape(depth, 1, nd))


def _ffn_kernel(*refs, has_pre, d_ff, tf):
    if has_pre:
        (x_ref, a_ref, wpre_ref, gpre_ref, sh_ref, sc_ref, g_ref, win_ref, wout_ref,
         o_ref, acc_ref) = refs
    else:
        x_ref, sh_ref, sc_ref, g_ref, win_ref, wout_ref, o_ref, acc_ref = refs
    x = x_ref[...]
    if has_pre:
        x = x + gpre_ref[0] * _dot(a_ref[...], wpre_ref[...])
    hb = _modulate(x, sh_ref[0], sc_ref[0]).astype(BF16)
    for c in range(d_ff // tf):
        u = _dot(hb, win_ref[:, c * tf:(c + 1) * tf])
        g = _dot(hb, win_ref[:, d_ff + c * tf:d_ff + (c + 1) * tf])
        act = ((u * _sigmoid(u)) * g).astype(BF16)
        y = _dot(act, wout_ref[c * tf:(c + 1) * tf, :])
        if c == 0:
            acc_ref[...] = y
        else:
            acc_ref[...] += y
    o_ref[...] = x + 0.5 * g_ref[0] * acc_ref[...]


def ffn_call(x, shift, scale, gate, w_in, w_out, pre=None, tm=FFN_ROW_TILE):
    n, d = x.shape
    tm = min(tm, n)
    groups = shift.shape[0]
    tiles_per_group = n // tm // groups
    d_ff = w_out.shape[0]
    tf = FF_CHUNK if d_ff % FF_CHUNK == 0 else d_ff

    def row_spec(width):
        return pl.BlockSpec((tm, width), lambda i: (i, 0))

    def mod_spec(m):
        return pl.BlockSpec((1,) + m.shape[1:], lambda i: (i // tiles_per_group, 0, 0))

    def whole(w):
        return pl.BlockSpec(w.shape, lambda i: (0, 0), pipeline_mode=pl.Buffered(1))

    args, specs = [x], [row_spec(d)]
    if pre is not None:
        a, w_pre, g_pre = pre
        args += [a, w_pre, g_pre]
        specs += [row_spec(a.shape[1]), whole(w_pre), mod_spec(g_pre)]
    args += [shift, scale, gate, w_in, w_out]
    specs += [mod_spec(shift), mod_spec(scale), mod_spec(gate), whole(w_in), whole(w_out)]
    return pl.pallas_call(
        functools.partial(_ffn_kernel, has_pre=pre is not None, d_ff=d_ff, tf=tf),
        grid=(n // tm,),
        in_specs=specs,
        out_specs=row_spec(d),
        out_shape=jax.ShapeDtypeStruct((n, d), F32),
        scratch_shapes=[pltpu.VMEM((tm, d), F32)],
        compiler_params=_params("arbitrary"),
        name="ffn_pre" if pre is not None else "ffn",
    )(*args)


def _head_mean_square(y, g_ref):
    y2 = (y * y).astype(BF16)
    w = g_ref.shape[0]
    parts = [_dot(y2[:, j * w:(j + 1) * w], g_ref[...]) for j in range(y.shape[1] // w)]
    return parts[0] if len(parts) == 1 else jnp.concatenate(parts, axis=1)


def _project(x, sh, sc, wq_ref, wk_ref, wv_ref, wqi_ref, wkw_ref, gq_ref, gk_ref, g_ref,
             qk_scale):
    hb = _modulate(x, sh, sc).astype(BF16)
    q = _dot(hb, wq_ref[...])
    qn = q * lax.rsqrt(_head_mean_square(q, g_ref) + RMS_EPS) * gq_ref[...] * qk_scale
    k = _dot(hb, wk_ref[...])
    kn = k * lax.rsqrt(_head_mean_square(k, g_ref) + RMS_EPS) * gk_ref[...]
    v = _dot(hb, wv_ref[...])
    qi = _dot(hb, wqi_ref[...])
    kw = _dot(hb, wkw_ref[...])
    return qn, kn, v, qi, kw


def _proj_kernel(x_ref, sh_ref, sc_ref, wq_ref, wk_ref, wv_ref, wqi_ref, wkw_ref, gq_ref,
                 gk_ref, g_ref,
                 qt_ref, k_ref, kb_ref, v_ref, vt_ref, qit_ref, ki_ref, kib_ref, wit_ref,
                 *, idx_dim, idx_heads, head_dim, v_rows, qk_scale, wi_scale):
    qn, kn, v, qi, kw = _project(x_ref[0], sh_ref[0], sc_ref[0], wq_ref, wk_ref, wv_ref,
                                 wqi_ref, wkw_ref, gq_ref, gk_ref, g_ref, qk_scale)
    tm = x_ref.shape[1]
    qt_ref[0, 0] = qn.T.astype(BF16)
    k_ref[0] = kn
    kb_ref[0] = kn.astype(BF16)
    v_ref[0] = v
    vt = v.T.astype(BF16)
    ones_rows = (lax.broadcasted_iota(I32, (v_rows - head_dim, tm), 0) == 0).astype(BF16)
    for g in range(v.shape[1] // head_dim):
        vt_ref[0, 0, g * v_rows:g * v_rows + head_dim, :] = vt[g * head_dim:(g + 1) * head_dim]
        vt_ref[0, 0, g * v_rows + head_dim:(g + 1) * v_rows, :] = ones_rows
    qit_ref[0, 0] = qi.T.astype(BF16)
    ki = kw[:, :idx_dim]
    ki_ref[0] = ki
    kib_ref[0] = ki.astype(BF16)
    wit_ref[0, 0] = kw.T[idx_dim:idx_dim + idx_heads, :] * wi_scale


def _sample_proj_kernel(x_ref, sh_ref, sc_ref, wq_ref, wk_ref, wv_ref, wqi_ref, wkw_ref,
                        gq_ref, gk_ref, g_ref,
                        q_ref, k_ref, v_ref, qi_ref, ki_ref, wi_ref,
                        *, idx_dim, idx_heads, qk_scale, wi_scale):
    qn, kn, v, qi, kw = _project(x_ref[...], sh_ref[0], sc_ref[0], wq_ref, wk_ref, wv_ref,
                                 wqi_ref, wkw_ref, gq_ref, gk_ref, g_ref, qk_scale)
    q_ref[...] = qn.astype(BF16)
    k_ref[...] = kn
    v_ref[...] = v
    qi_ref[...] = qi.astype(BF16)
    ki_ref[...] = kw[:, :idx_dim]
    wi_ref[...] = kw[:, idx_dim:idx_dim + idx_heads] * wi_scale


def _proj_weight_specs(ws, n_grid):
    zeros = (lambda *idx: (0, 0))
    return [pl.BlockSpec(w.shape, zeros, pipeline_mode=pl.Buffered(1)) for w in ws]


def attn_proj_prompt(x3, shift, scale, aw, dims, tm=ROW_TILE):
    b, t, d = x3.shape
    nt = t // tm
    hd, kvd, ihd, di, ih = dims["hd"], dims["kvd"], dims["ihd"], dims["idx_dim"], dims["idx_heads"]
    ws = [aw["wq"], aw["wk"], aw["wv"], aw["wqi"], aw["wkw"], aw["gq"], aw["gk"], aw["g"]]
    mod_spec = pl.BlockSpec((1, 1, d), lambda bi, i: (bi, 0, 0))
    nat = lambda w: pl.BlockSpec((1, tm, w), lambda bi, i: (bi, i, 0))
    tr = lambda r: pl.BlockSpec((1, 1, r, tm), lambda bi, i: (bi, i, 0, 0))
    vtd = kvd // dims["head_dim"] * dims["v_rows"]
    return pl.pallas_call(
        functools.partial(_proj_kernel, idx_dim=di, idx_heads=ih, head_dim=dims["head_dim"],
                          v_rows=dims["v_rows"], qk_scale=dims["qk_scale"] * LOG2_E,
                          wi_scale=dims["wi_scale"]),
        grid=(b, nt),
        in_specs=[nat(d), mod_spec, mod_spec] + _proj_weight_specs(ws, 2),
        out_specs=[tr(hd), nat(kvd), nat(kvd), nat(kvd), tr(vtd), tr(ihd), nat(di), nat(di), tr(ih)],
        out_shape=[
            jax.ShapeDtypeStruct((b, nt, hd, tm), BF16),
            jax.ShapeDtypeStruct((b, t, kvd), F32),
            jax.ShapeDtypeStruct((b, t, kvd), BF16),
            jax.ShapeDtypeStruct((b, t, kvd), F32),
            jax.ShapeDtypeStruct((b, nt, vtd, tm), BF16),
            jax.ShapeDtypeStruct((b, nt, ihd, tm), BF16),
            jax.ShapeDtypeStruct((b, t, di), F32),
            jax.ShapeDtypeStruct((b, t, di), BF16),
            jax.ShapeDtypeStruct((b, nt, ih, tm), F32),
        ],
        compiler_params=_params("arbitrary", "arbitrary"),
        name="attn_proj",
    )(x3, shift, scale, *ws)


def attn_proj_sample(x, shift, scale, aw, dims):
    n, d = x.shape
    hd, kvd, ihd, di, ih = dims["hd"], dims["kvd"], dims["ihd"], dims["idx_dim"], dims["idx_heads"]
    ws = [aw["wq"], aw["wk"], aw["wv"], aw["wqi"], aw["wkw"], aw["gq"], aw["gk"], aw["g"]]
    full = lambda w: pl.BlockSpec((n, w), lambda i: (0, 0))
    mod_spec = pl.BlockSpec((1, n, d), lambda i: (0, 0, 0))
    return pl.pallas_call(
        functools.partial(_sample_proj_kernel, idx_dim=di, idx_heads=ih,
                          qk_scale=dims["qk_scale"], wi_scale=dims["wi_scale"]),
        grid=(1,),
        in_specs=[full(d), mod_spec, mod_spec] + _proj_weight_specs(ws, 1),
        out_specs=[full(hd), full(kvd), full(kvd), full(ihd), full(di), full(ih)],
        out_shape=[
            jax.ShapeDtypeStruct((n, hd), BF16),
            jax.ShapeDtypeStruct((n, kvd), F32),
            jax.ShapeDtypeStruct((n, kvd), F32),
            jax.ShapeDtypeStruct((n, ihd), BF16),
            jax.ShapeDtypeStruct((n, di), F32),
            jax.ShapeDtypeStruct((n, ih), F32),
        ],
        compiler_params=_params("arbitrary"),
        name="attn_proj_sample",
    )(x, shift, scale, *ws)


def _dsa_kernel(qt_ref, qit_ref, wit_ref, k_ref, vt_ref, ki_ref, o_ref,
                keys_ref, hi_ref, lo_ref, qpad_ref, s_ref, acc_ref, m_ref, mnew_ref, l_ref, jlim_ref,
                *, n_heads, group, head_dim, v_rows, idx_heads, idx_dim, topk, seq_bits):
    tq = qt_ref.shape[-1]
    tk = tq
    tk2 = 2 * tk
    i = pl.program_id(1)
    n_chunks = i + 1
    n_pairs = (n_chunks + 1) // 2
    q_idx = i * tq + lax.broadcasted_iota(I32, (tk, tq), 1)
    row_iota = lax.broadcasted_iota(I32, (tk, tq), 0)
    q_idx2 = i * tq + lax.broadcasted_iota(I32, (tk2, tq), 1)
    row_iota2 = lax.broadcasted_iota(I32, (tk2, tq), 0)

    def chunk_rows(c):
        return pl.ds(pl.multiple_of(c * tk, tk), tk)

    def pair_rows(j):
        return pl.ds(pl.multiple_of(j * tk2, tk2), tk2)

    def score_chunk(c, carry):
        kic = ki_ref[0, chunk_rows(c), :]
        acc = jnp.zeros((tk, tq), F32)
        for h in range(idx_heads):
            s = _dot(kic, qit_ref[0, 0, h * idx_dim:(h + 1) * idx_dim, :])
            acc = acc + jnp.maximum(s, 0.0) * wit_ref[0, 0, h:h + 1, :]
        acc = jnp.where(c * tk + row_iota <= q_idx, acc, NEG_INF)
        key = _float_key(acc)
        keys_ref[chunk_rows(c), :] = key
        hi_ref[chunk_rows(c), :] = lax.shift_right_arithmetic(key, 16).astype(I16)
        lo_ref[chunk_rows(c), :] = ((key & 0xFFFF) - HALF16).astype(I16)
        return carry

    lax.fori_loop(0, n_chunks, score_chunk, 0)

    @pl.when(n_chunks % 2 == 1)
    def _():
        keys_ref[chunk_rows(n_chunks), :] = jnp.full((tk, tq), INT_MIN, I32)
        hi_ref[chunk_rows(n_chunks), :] = jnp.full((tk, tq), INT16_MIN, I16)
        lo_ref[chunk_rows(n_chunks), :] = jnp.full((tk, tq), INT16_MIN, I16)

    def count(pred):
        def body(j, cnt):
            kidx = j * tk2 + row_iota2
            return cnt + _fold_rows(pred(keys_ref[pair_rows(j), :], kidx).astype(I32), jnp.add)
        cnt8 = lax.fori_loop(0, n_pairs, body, jnp.zeros((SUBLANES, tq), I32))
        return jnp.sum(cnt8, axis=0, keepdims=True)

    def count_ge16(ref, cand):
        cand16 = cand.astype(I16)
        def body(j, cnt):
            m = (ref[pair_rows(j), :] >= cand16).astype(I16)
            return cnt + _fold_rows(m, jnp.add, BF16_SUBLANES).astype(I32)
        cnt16 = lax.fori_loop(0, n_pairs, body, jnp.zeros((BF16_SUBLANES, tq), I32))
        return jnp.sum(cnt16, axis=0, keepdims=True)

    def bisect16(ref, k_need, n_all):
        zero = jnp.zeros((1, tq), I32)
        c0 = count_ge16(ref, zero)
        nonneg = c0 >= k_need
        init = (jnp.where(nonneg, zero, INT16_MIN), jnp.where(nonneg, c0, n_all),
                jnp.where(nonneg, zero, c0))

        def search(it, carry):
            prefix, n_ge, n_up = carry
            cand = prefix | jnp.left_shift(jnp.int32(1), 14 - it)
            c = count_ge16(ref, cand)
            ok = c >= k_need
            return jnp.where(ok, cand, prefix), jnp.where(ok, c, n_ge), jnp.where(ok, n_up, c)

        return lax.fori_loop(0, 15, search, init)

    zero = jnp.zeros((1, tq), I32)
    thr_hi, n_ge_hi, n_gt_hi = bisect16(hi_ref, topk, n_chunks * tk)

    thr_hi16 = thr_hi.astype(I16)

    def mask_low(j, carry):
        keep = hi_ref[pair_rows(j), :] == thr_hi16
        lo_ref[pair_rows(j), :] = jnp.where(keep, lo_ref[pair_rows(j), :], INT16_MIN).astype(I16)
        return carry

    lax.fori_loop(0, n_pairs, mask_low, 0)
    thr_lo, n_ge_lo, n_gt_lo = bisect16(lo_ref, topk - n_gt_hi, n_ge_hi - n_gt_hi)
    thr = lax.shift_left(thr_hi, 16) | (thr_lo + HALF16)
    n_ge = n_gt_hi + n_ge_lo
    n_gt = n_gt_hi + n_gt_lo
    need = topk - n_gt

    jlim_ref[...] = jnp.full((1, tq), (1 << seq_bits) - 1, I32)
    excess = jnp.logical_and(n_ge > topk, thr > KEY_NEG_INF)

    @pl.when(jnp.max(excess.astype(I32)) > 0)
    def _():
        def tie_search(it, j):
            cand = j | jnp.left_shift(jnp.int32(1), seq_bits - 1 - it)
            n_before = count(lambda kc, kidx: jnp.logical_and(kc == thr, kidx < cand))
            return jnp.where(n_before < need, cand, j)
        jlim_ref[...] = lax.fori_loop(0, seq_bits, tie_search, zero)

    jlim = jlim_ref[...]

    def bias_pair(j, carry):
        kc = keys_ref[pair_rows(j), :]
        kidx = j * tk2 + row_iota2
        sel = jnp.logical_or(kc > thr, jnp.logical_and(kc == thr, kidx <= jlim))
        sel = jnp.logical_and(sel, kidx <= q_idx2)
        bias = jnp.where(sel, 0.0, NEG_INF).astype(F32)
        keys_ref[pair_rows(j), :] = lax.bitcast_convert_type(bias, I32)
        return carry

    lax.fori_loop(0, n_pairs, bias_pair, 0)

    qpad_ref[...] = jnp.zeros(qpad_ref.shape, BF16)
    for h in range(n_heads):
        g = h // group
        qpad_ref[h, g * head_dim:(g + 1) * head_dim, :] = qt_ref[0, 0, h * head_dim:(h + 1) * head_dim, :]
    m_ref[...] = jnp.full(m_ref.shape, NEG_INF, F32)
    l_ref[...] = jnp.zeros(l_ref.shape, F32)
    acc_ref[...] = jnp.zeros(acc_ref.shape, F32)

    def logits_head(h, kc, bias, buf, m_prev):
        s = _dot(kc, qpad_ref[h]) + bias
        s_ref[buf, h] = s
        mnew_ref[buf, h:h + 1, :] = jnp.maximum(
            m_prev, jnp.max(_fold_rows(s, jnp.maximum), axis=0, keepdims=True))

    def weights_head(h, c, buf):
        g = h // group
        rows = slice(h * head_dim, (h + 1) * head_dim)
        m_old = m_ref[h:h + 1, :]
        m_new = mnew_ref[buf, h:h + 1, :]
        m_safe = jnp.where(m_new == NEG_INF, 0.0, m_new)
        p = jnp.exp2(s_ref[buf, h] - m_safe).astype(BF16)
        alpha = jnp.exp2(m_old - m_safe)
        pv = _dot(vt_ref[0, c, g * v_rows:(g + 1) * v_rows, :], p)
        acc_ref[rows, :] = alpha * acc_ref[rows, :] + pv[0:head_dim]
        l_ref[h:h + 1, :] = alpha * l_ref[h:h + 1, :] + pv[head_dim:head_dim + 1]
        m_ref[h:h + 1, :] = m_new

    def chunk_operands(c):
        return k_ref[0, chunk_rows(c), :], lax.bitcast_convert_type(keys_ref[chunk_rows(c), :], F32)

    kc0, bias0 = chunk_operands(0)
    for h in range(n_heads):
        logits_head(h, kc0, bias0, 0, m_ref[h:h + 1, :])

    def step(c, src, dst):
        kc, bias = chunk_operands(c + 1)
        for h in range(n_heads):
            logits_head(h, kc, bias, dst, mnew_ref[src, h:h + 1, :])
            weights_head(h, c, src)

    def last_weights(buf):
        for h in range(n_heads):
            weights_head(h, n_chunks - 1, buf)

    def two_steps(j, carry):
        step(2 * j, 0, 1)
        step(2 * j + 1, 1, 0)
        return carry

    n_steps = n_chunks - 1
    lax.fori_loop(0, n_steps // 2, two_steps, 0)

    @pl.when(n_steps % 2 == 1)
    def _():
        step(n_steps - 1, 0, 1)
        last_weights(1)

    @pl.when(n_steps % 2 == 0)
    def _():
        last_weights(0)

    for h in range(n_heads):
        rows = slice(h * head_dim, (h + 1) * head_dim)
        acc_ref[rows, :] = acc_ref[rows, :] / l_ref[h:h + 1, :]
    o_ref[0] = acc_ref[...].T.astype(BF16)


def dsa_prompt(qt, qit, wit, kb, vt, kib, dims):
    b, nt, hd, tq = qt.shape
    t = nt * tq
    kvd, ihd, ih = kb.shape[-1], qit.shape[2], wit.shape[2]
    nh = dims["n_heads"]
    topk = min(TOPK_MAX, t // 4)
    blk = lambda r: pl.BlockSpec((1, 1, r, tq), lambda bi, i: (bi, i, 0, 0))
    per_batch3 = lambda w: pl.BlockSpec((1, t, w), lambda bi, i: (bi, 0, 0),
                                        pipeline_mode=pl.Buffered(1))
    return pl.pallas_call(
        functools.partial(_dsa_kernel, n_heads=nh, group=dims["group"],
                          head_dim=dims["head_dim"], v_rows=dims["v_rows"], idx_heads=ih,
                          idx_dim=dims["idx_dim"], topk=topk, seq_bits=max(1, (t - 1).bit_length())),
        grid=(b, nt),
        in_specs=[blk(hd), blk(ihd), blk(ih), per_batch3(kvd),
                  pl.BlockSpec((1, nt) + vt.shape[2:], lambda bi, i: (bi, 0, 0, 0),
                               pipeline_mode=pl.Buffered(1)),
                  per_batch3(dims["idx_dim"])],
        out_specs=pl.BlockSpec((1, tq, hd), lambda bi, i: (bi, i, 0)),
        out_shape=jax.ShapeDtypeStruct((b, t, hd), BF16),
        scratch_shapes=[pltpu.VMEM((t + tq, tq), I32), pltpu.VMEM((t + tq, tq), I16),
                        pltpu.VMEM((t + tq, tq), I16), pltpu.VMEM((nh, kvd, tq), BF16),
                        pltpu.VMEM((2, nh, tq, tq), F32), pltpu.VMEM((hd, tq), F32),
                        pltpu.VMEM((nh, tq), F32), pltpu.VMEM((2, nh, tq), F32),
                        pltpu.VMEM((nh, tq), F32), pltpu.VMEM((1, tq), I32)],
        compiler_params=_params("arbitrary", "arbitrary"),
        name="dsa_prompt",
    )(qt, qit, wit, kb, vt, kib)


def _dsa_sample_kernel(pt_ref, qpad_ref, hsel_ref, qi_ref, wi_ref, knew_ref, vnew_ref, kinew_ref,
                       ckt_ref, cvt_ref, ckit_ref, o_ref,
                       ktbuf, vtbuf, kitbuf, sem,
                       *, layer, n_pages, page, topk, total_bits, n_groups, head_dim, key_chunk):
    bi = pl.program_id(0)
    n_seq = pl.num_programs(0)
    past = n_pages * page
    slot = bi % 2

    def page_copies(seq, buf, p):
        phys = pt_ref[seq, p]
        cols = pl.ds(pl.multiple_of(p * page, page), page)
        return (pltpu.make_async_copy(ckt_ref.at[layer, phys], ktbuf.at[buf, :, cols], sem.at[buf, 0]),
                pltpu.make_async_copy(cvt_ref.at[layer, phys], vtbuf.at[buf, :, cols], sem.at[buf, 1]),
                pltpu.make_async_copy(ckit_ref.at[layer, phys], kitbuf.at[buf, :, cols], sem.at[buf, 2]))

    def start_seq(seq, buf):
        def body(p, carry):
            for cp in page_copies(seq, buf, p):
                cp.start()
            return carry
        lax.fori_loop(0, n_pages, body, 0)

    def wait_seq(seq, buf):
        def body(p, carry):
            for cp in page_copies(seq, buf, p):
                cp.wait()
            return carry
        lax.fori_loop(0, n_pages, body, 0)

    @pl.when(bi == 0)
    def _():
        start_seq(0, 0)

    @pl.when(bi + 1 < n_seq)
    def _():
        start_seq(bi + 1, 1 - slot)

    wait_seq(bi, slot)
    chunks = [slice(c * key_chunk, (c + 1) * key_chunk) for c in range(past // key_chunk)]

    qi = qi_ref[0]
    wi = wi_ref[0]
    score = jnp.concatenate(
        [jnp.sum(jnp.maximum(_dot(qi, kitbuf[slot, :, ch].astype(BF16)), 0.0) * wi, axis=0,
                 keepdims=True) for ch in chunks], axis=1) + 0.0
    s_new = jnp.sum(qi.astype(F32) * kinew_ref[0], axis=1, keepdims=True)
    score_new = jnp.sum(jnp.maximum(s_new, 0.0) * wi, axis=0, keepdims=True) + 0.0
    keys = _float_key(score)
    key_new = _float_key(score_new)
    kidx = lax.broadcasted_iota(I32, (1, past), 1)

    def count(pred):
        n = jnp.sum(pred(keys, kidx).astype(I32), axis=1, keepdims=True)
        return n + pred(key_new, jnp.full((1, 1), past, I32)).astype(I32)

    zero = jnp.zeros((1, 1), I32)
    prefix = jnp.where(count(lambda k, j: k >= zero) >= topk, zero, INT_MIN)

    def search(it, prefix):
        cand = prefix | jnp.left_shift(jnp.int32(1), 30 - it)
        return jnp.where(count(lambda k, j: k >= cand) >= topk, cand, prefix)

    thr = lax.fori_loop(0, 31, search, prefix)
    need = topk - count(lambda k, j: k > thr)

    def tie_search(it, jl):
        cand = jl | jnp.left_shift(jnp.int32(1), total_bits - 1 - it)
        n_before = count(lambda k, j: jnp.logical_and(k == thr, j < cand))
        return jnp.where(n_before < need, cand, jl)

    jlim = lax.fori_loop(0, total_bits, tie_search, zero)

    def selected(k, j):
        return jnp.logical_or(k > thr, jnp.logical_and(k == thr, j <= jlim))

    sel = selected(keys, kidx)
    sel_new = selected(key_new, jnp.full((1, 1), past, I32))

    qpad = qpad_ref[0]
    logits = jnp.concatenate([_dot(qpad, ktbuf[slot, :, ch].astype(BF16)) for ch in chunks], axis=1)
    logits = jnp.where(sel, logits, NEG_INF)
    logit_new = jnp.sum(qpad.astype(F32) * knew_ref[0], axis=1, keepdims=True)
    logit_new = jnp.where(sel_new, logit_new, NEG_INF)
    m = jnp.maximum(jnp.max(logits, axis=1, keepdims=True), logit_new)
    p = jnp.exp(logits - m)
    p_new = jnp.exp(logit_new - m)
    denom = jnp.sum(p, axis=1, keepdims=True) + p_new
    pb = p.astype(BF16)
    o_all = p_new * vnew_ref[0]
    for ch in chunks:
        o_all = o_all + _dot_nt(pb[:, ch], vtbuf[slot, :, ch].astype(BF16))
    o_all = o_all / denom
    out = jnp.zeros((o_all.shape[0], head_dim), F32)
    for g in range(n_groups):
        out = out + hsel_ref[g] * o_all[:, g * head_dim:(g + 1) * head_dim]
    o_ref[0] = out.astype(BF16)


def dsa_sample(q, k_new, v_new, qi, ki_new, wi, cache_kt, cache_vt, cache_kit, layer, page_table, dims):
    db, hd = q.shape
    n_layers, n_pool, kvd, page = cache_kt.shape
    dh = dims["head_dim"]
    kvh = kvd // dh
    n_pages = page_table.shape[1]
    past = n_pages * page
    nh, group, ih, di = dims["n_heads"], dims["group"], dims["idx_heads"], dims["idx_dim"]
    topk = min(TOPK_MAX, (past + 1) // 4)
    key_chunk = SAMPLE_KEY_CHUNK if past % SAMPLE_KEY_CHUNK == 0 else past
    head_group = jnp.arange(nh) // group
    onehot = (head_group[:, None] == jnp.arange(kvh)[None, :])
    qpad = (q.reshape(db, nh, 1, dh) * onehot[None, :, :, None].astype(q.dtype)).reshape(db, nh, kvd)
    hsel = jnp.transpose(onehot.astype(F32))[:, :, None]
    row = lambda w: pl.BlockSpec((1, 1, w), lambda bi, pt: (bi, 0, 0))
    any_spec = pl.BlockSpec(memory_space=pl.ANY)
    grid_spec = pltpu.PrefetchScalarGridSpec(
        num_scalar_prefetch=1,
        grid=(db,),
        in_specs=[
            pl.BlockSpec((1, nh, kvd), lambda bi, pt: (bi, 0, 0)),
            pl.BlockSpec((kvh, nh, 1), lambda bi, pt: (0, 0, 0)),
            pl.BlockSpec((1, ih, di), lambda bi, pt: (bi, 0, 0)),
            pl.BlockSpec((1, ih, 1), lambda bi, pt: (bi, 0, 0)),
            row(kvd), row(kvd), row(di),
            any_spec, any_spec, any_spec,
        ],
        out_specs=pl.BlockSpec((1, nh, dh), lambda bi, pt: (bi, 0, 0)),
        scratch_shapes=[pltpu.VMEM((2, kvd, past), F32), pltpu.VMEM((2, kvd, past), F32),
                        pltpu.VMEM((2, di, past), F32), pltpu.SemaphoreType.DMA((2, 3))],
    )
    out = pl.pallas_call(
        functools.partial(_dsa_sample_kernel, layer=layer, n_pages=n_pages, page=page, topk=topk,
                          total_bits=max(1, past.bit_length()), n_groups=kvh, head_dim=dh,
                          key_chunk=key_chunk),
        grid_spec=grid_spec,
        out_shape=jax.ShapeDtypeStruct((db, nh, dh), BF16),
        compiler_params=_params("arbitrary"),
        name="dsa_sample",
    )(page_table, qpad, hsel, qi.reshape(db, ih, di), wi.reshape(db, ih, 1),
      k_new.reshape(db, 1, kvd), v_new.reshape(db, 1, kvd), ki_new.reshape(db, 1, di),
      cache_kt, cache_vt, cache_kit)
    return out.reshape(db, hd)


def _conv_prompt_kernel(x_ref, sh_ref, sc_ref, win_ref, cw_ref, a_ref, st_ref, carry_ref):
    i = pl.program_id(1)
    d = x_ref.shape[-1]
    tm = x_ref.shape[1]
    hb = _modulate(x_ref[0], sh_ref[0], sc_ref[0]).astype(BF16)
    b_gate = _dot(hb, win_ref[:, 0:d])
    z = _dot(hb, win_ref[:, d:2 * d]) * _dot(hb, win_ref[:, 2 * d:3 * d])

    @pl.when(i == 0)
    def _():
        carry_ref[...] = jnp.zeros_like(carry_ref)

    row = lax.broadcasted_iota(I32, (tm, d), 0)
    prev1 = carry_ref[SUBLANES - 1:SUBLANES, :]
    prev2 = carry_ref[SUBLANES - 2:SUBLANES - 1, :]
    z1 = jnp.where(row == 0, prev1, pltpu.roll(z, 1, axis=0))
    z2 = jnp.where(row == 0, prev2, jnp.where(row == 1, prev1, pltpu.roll(z, 2, axis=0)))
    y = cw_ref[0:1, :] * z2 + cw_ref[1:2, :] * z1 + cw_ref[2:3, :] * z
    a_ref[0] = (b_gate * y).astype(BF16)
    tail = z[tm - SUBLANES:tm, :]
    carry_ref[...] = tail
    st_ref[0] = tail


def conv_prompt(x3, shift, scale, w_in, conv_w8, tm=ROW_TILE):
    b, t, d = x3.shape
    mod_spec = pl.BlockSpec((1, 1, d), lambda bi, i: (bi, 0, 0))
    return pl.pallas_call(
        _conv_prompt_kernel,
        grid=(b, t // tm),
        in_specs=[pl.BlockSpec((1, tm, d), lambda bi, i: (bi, i, 0)), mod_spec, mod_spec,
                  pl.BlockSpec(w_in.shape, lambda bi, i: (0, 0), pipeline_mode=pl.Buffered(1)),
                  pl.BlockSpec(conv_w8.shape, lambda bi, i: (0, 0))],
        out_specs=[pl.BlockSpec((1, tm, d), lambda bi, i: (bi, i, 0)),
                   pl.BlockSpec((1, SUBLANES, d), lambda bi, i: (bi, 0, 0))],
        out_shape=[jax.ShapeDtypeStruct((b, t, d), BF16),
                   jax.ShapeDtypeStruct((b, SUBLANES, d), F32)],
        scratch_shapes=[pltpu.VMEM((SUBLANES, d), F32)],
        compiler_params=_params("arbitrary", "arbitrary"),
        name="conv_prompt",
    )(x3, shift, scale, w_in, conv_w8)


def _conv_sample_kernel(x_ref, sh_ref, sc_ref, win_ref, cw_ref, p0_ref, p1_ref, a_ref, z_ref):
    d = x_ref.shape[-1]
    hb = _modulate(x_ref[...], sh_ref[0], sc_ref[0]).astype(BF16)
    b_gate = _dot(hb, win_ref[:, 0:d])
    z = _dot(hb, win_ref[:, d:2 * d]) * _dot(hb, win_ref[:, 2 * d:3 * d])
    y = cw_ref[0:1, :] * p0_ref[...] + cw_ref[1:2, :] * p1_ref[...] + cw_ref[2:3, :] * z
    a_ref[...] = (b_gate * y).astype(BF16)
    z_ref[...] = z


def conv_sample(x, shift, scale, w_in, conv_w8, prefix0, prefix1):
    n, d = x.shape
    full = pl.BlockSpec((n, d), lambda i: (0, 0))
    mod_spec = pl.BlockSpec((1, n, d), lambda i: (0, 0, 0))
    return pl.pallas_call(
        _conv_sample_kernel,
        grid=(1,),
        in_specs=[full, mod_spec, mod_spec,
                  pl.BlockSpec(w_in.shape, lambda i: (0, 0), pipeline_mode=pl.Buffered(1)),
                  pl.BlockSpec(conv_w8.shape, lambda i: (0, 0)), full, full],
        out_specs=[full, full],
        out_shape=[jax.ShapeDtypeStruct((n, d), BF16), jax.ShapeDtypeStruct((n, d), F32)],
        compiler_params=_params("arbitrary"),
        name="conv_sample",
    )(x, shift, scale, w_in, conv_w8, prefix0, prefix1)


def kernel(x_prompt, x_sample, cache_k, cache_v, cache_kidx, state_conv, page_table, c_prompt, c_sample, w_ada, b_ada, w_ffn_in, w_ffn_out, w_attn_in, w_attn_out, q_norm_gain, k_norm_gain, w_conv_in, conv_w, w_conv_out):
    b, t, d = x_prompt.shape
    db, dt, _ = x_sample.shape
    assert dt == 1, "the sample path handles one new token per sequence"
    depth = w_ada.shape[0]
    head_dim = q_norm_gain.shape[-1]
    n_heads = w_attn_out.shape[1] // head_dim
    kvh = cache_k.shape[3]
    idx_dim = cache_kidx.shape[-1]
    hd, kvd = n_heads * head_dim, kvh * head_dim
    idx_heads = (w_attn_in.shape[-1] - hd - 2 * kvd - idx_dim) // (idx_dim + 1)
    ihd = idx_heads * idx_dim
    conv_width = conv_w.shape[1]
    assert conv_width == 3
    dims = dict(n_heads=n_heads, group=n_heads // kvh, head_dim=head_dim, hd=hd, kvd=kvd, ihd=ihd,
                idx_dim=idx_dim, idx_heads=idx_heads, v_rows=head_dim + BF16_SUBLANES,
                qk_scale=head_dim ** -0.5,
                wi_scale=idx_heads ** -0.5 * idx_dim ** -0.5)

    n_c = b + db
    n_c_pad = -(-n_c // SUBLANES) * SUBLANES
    c_all = jnp.concatenate([c_prompt, c_sample, jnp.zeros((n_c_pad - n_c, d), F32)], axis=0)
    mods = ada_mods(c_all, w_ada, b_ada).reshape(depth, n_c_pad, N_MOD, d)

    def prompt_mod(layer, j):
        return mods[layer, :b, j][:, None, :]

    def sample_mod(layer, j):
        return mods[layer, b:n_c, j][None]

    gw = 256 if kvd % 256 == 0 and hd % 256 == 0 else kvd
    gi = jnp.arange(gw) // head_dim
    g_mat = jnp.where(gi[:, None] == gi[None, :], 1.0 / head_dim, 0.0).astype(BF16)

    n_attn, n_pool, page = cache_k.shape[:3]
    cache_kt = jnp.transpose(cache_k, (0, 1, 3, 4, 2)).reshape(n_attn, n_pool, kvd, page)
    cache_vt = jnp.transpose(cache_v, (0, 1, 3, 4, 2)).reshape(n_attn, n_pool, kvd, page)
    cache_kit = jnp.transpose(cache_kidx, (0, 1, 3, 2))

    xp = x_prompt.reshape(b * t, d)
    xs = x_sample.reshape(db, d)
    pk, pv, pki, pconv, sk, sv, ski, sconv = [], [], [], [], [], [], [], []
    n_mixers = 2
    for layer in range(depth):
        li = layer // n_mixers
        w1_in, w1_out = w_ffn_in[layer, 0].astype(BF16), w_ffn_out[layer, 0].astype(BF16)
        w2_in, w2_out = w_ffn_in[layer, 1].astype(BF16), w_ffn_out[layer, 1].astype(BF16)
        pm = [prompt_mod(layer, j) for j in range(N_MOD)]
        sm = [sample_mod(layer, j) for j in range(N_MOD)]
        xp = ffn_call(xp, pm[0], pm[1], pm[2], w1_in, w1_out)
        xs = ffn_call(xs, sm[0], sm[1], sm[2], w1_in, w1_out)
        if layer % n_mixers == 0:
            wa = w_attn_in[li].astype(BF16)
            o1, o2, o3, o4, o5 = hd, hd + kvd, hd + 2 * kvd, hd + 2 * kvd + ihd, hd + 2 * kvd + ihd + idx_dim
            wkw = jnp.pad(wa[:, o4:], ((0, 0), (0, 128 - (idx_dim + idx_heads))))
            aw = dict(wq=wa[:, :o1], wk=wa[:, o1:o2], wv=wa[:, o2:o3], wqi=wa[:, o3:o4], wkw=wkw,
                      gq=jnp.tile(q_norm_gain[li], n_heads)[None, :],
                      gk=jnp.tile(k_norm_gain[li], kvh)[None, :], g=g_mat)
            w_mix_out = w_attn_out[li].astype(BF16)
            qt, k, kb, v, vt, qit, ki, kib, wit = attn_proj_prompt(xp.reshape(b, t, d), pm[3], pm[4], aw, dims)
            ap = dsa_prompt(qt, qit, wit, kb, vt, kib, dims).reshape(b * t, hd)
            pk.append(k.reshape(b, t, kvh, head_dim))
            pv.append(v.reshape(b, t, kvh, head_dim))
            pki.append(ki)
            qs, ks, vs, qis, kis, wis = attn_proj_sample(xs, sm[3], sm[4], aw, dims)
            a_s = dsa_sample(qs, ks, vs, qis, kis, wis, cache_kt, cache_vt, cache_kit, li,
                             page_table, dims)
            sk.append(ks.reshape(db, 1, kvh, head_dim))
            sv.append(vs.reshape(db, 1, kvh, head_dim))
            ski.append(kis.reshape(db, 1, idx_dim))
        else:
            wc_in = w_conv_in[li].astype(BF16)
            w_mix_out = w_conv_out[li].astype(BF16)
            cw8 = jnp.pad(conv_w[li], ((0, SUBLANES - conv_width), (0, 0)))
            ap, st = conv_prompt(xp.reshape(b, t, d), pm[3], pm[4], wc_in, cw8)
            ap = ap.reshape(b * t, d)
            pconv.append(st[:, SUBLANES - (conv_width - 1):, :])
            prefix = state_conv[li]
            a_s, zs = conv_sample(xs, sm[3], sm[4], wc_in, cw8, prefix[:, 0], prefix[:, 1])
            sconv.append(jnp.stack([prefix[:, 1], zs], axis=1))
        xp = ffn_call(xp, pm[6], pm[7], pm[8], w2_in, w2_out, pre=(ap, w_mix_out, pm[5]))
        xs = ffn_call(xs, sm[6], sm[7], sm[8], w2_in, w2_out, pre=(a_s, w_mix_out, sm[5]))

    return (xp.reshape(b, t, d), xs.reshape(db, 1, d),
            jnp.stack(pk), jnp.stack(pv), jnp.stack(pki), jnp.stack(pconv),
            jnp.stack(sk), jnp.stack(sv), jnp.stack(ski), jnp.stack(sconv))
```

```python
import functools

import jax
import jax.numpy as jnp
from jax import lax
from jax.experimental import pallas as pl
from jax.experimental.pallas import tpu as pltpu

F32 = jnp.float32
BF16 = jnp.bfloat16
I32 = jnp.int32
I16 = jnp.int16

RMS_EPS = 1e-6
TOPK_MAX = 256
N_MOD = 9
SUBLANES = 8
BF16_SUBLANES = 16
LOG2_E = 1.4426950408889634
NEG_INF = float("-inf")
INT_MIN = -(2 ** 31)
INT16_MIN = -(2 ** 15)
HALF16 = 2 ** 15
KEY_NEG_INF = (0xFF800000 ^ 0x7FFFFFFF) - (1 << 32)

VMEM_LIMIT_BYTES = 56 * 1024 * 1024
ROW_TILE = 256
FFN_ROW_TILE = 512
FF_CHUNK = 256
SAMPLE_KEY_CHUNK = 2048
SELECT_CHUNKS = 2
RADIX_BITS = 4


def _params(*sem):
    return pltpu.CompilerParams(dimension_semantics=sem, vmem_limit_bytes=VMEM_LIMIT_BYTES)


def _sigmoid(x):
    return 1.0 / (1.0 + jnp.exp(-x))


def _modulate(x, shift, scale):
    ms = jnp.mean(x * x, axis=-1, keepdims=True)
    return (x * lax.rsqrt(ms + RMS_EPS)) * (1.0 + scale) + shift


def _dot(a, b):
    return jnp.dot(a, b, preferred_element_type=F32)


def _dot_nt(a, b):
    return lax.dot_general(a, b, (((1,), (1,)), ((), ())), preferred_element_type=F32)


def _float_key(x):
    bits = lax.bitcast_convert_type(x, I32)
    return bits ^ (lax.shift_right_arithmetic(bits, 31) & 0x7FFFFFFF)


def _fold_rows(x, op, rows=SUBLANES):
    parts = [x[j * rows:(j + 1) * rows] for j in range(x.shape[0] // rows)]
    while len(parts) > 1:
        parts = [op(parts[j], parts[j + 1]) if j + 1 < len(parts) else parts[j]
                 for j in range(0, len(parts), 2)]
    return parts[0]


def _ada_kernel(c_ref, w_ref, b_ref, o_ref):
    c = c_ref[...]
    s = (c * _sigmoid(c)).astype(BF16)
    o_ref[0] = _dot(s, w_ref[0].astype(BF16)) + b_ref[0]


def ada_mods(c_all, w_ada, b_ada):
    depth, d, nd = w_ada.shape
    r = c_all.shape[0]
    return pl.pallas_call(
        _ada_kernel,
        grid=(depth, nd // d),
        in_specs=[
            pl.BlockSpec((r, d), lambda l, j: (0, 0)),
            pl.BlockSpec((1, d, d), lambda l, j: (l, 0, j)),
            pl.BlockSpec((1, 1, d), lambda l, j: (l, 0, j)),
        ],
        out_specs=pl.BlockSpec((1, r, d), lambda l, j: (l, 0, j)),
        out_shape=jax.ShapeDtypeStruct((depth, r, nd), F32),
        compiler_params=_params("arbitrary", "arbitrary"),
        name="ada_mods",
    )(c_all, w_ada, b_ada.reshape(depth, 1, nd))


def _ffn_kernel(*refs, has_pre, d_ff, tf):
    if has_pre:
        (x_ref, a_ref, wpre_ref, gpre_ref, sh_ref, sc_ref, g_ref, win_ref, wout_ref,
         o_ref, acc_ref) = refs
    else:
        x_ref, sh_ref, sc_ref, g_ref, win_ref, wout_ref, o_ref, acc_ref = refs
    x = x_ref[...]
    if has_pre:
        x = x + gpre_ref[0] * _dot(a_ref[...], wpre_ref[...])
    hb = _modulate(x, sh_ref[0], sc_ref[0]).astype(BF16)
    for c in range(d_ff // tf):
        u = _dot(hb, win_ref[:, c * tf:(c + 1) * tf])
        g = _dot(hb, win_ref[:, d_ff + c * tf:d_ff + (c + 1) * tf])
        act = ((u * _sigmoid(u)) * g).astype(BF16)
        y = _dot(act, wout_ref[c * tf:(c + 1) * tf, :])
        if c == 0:
            acc_ref[...] = y
        else:
            acc_ref[...] += y
    o_ref[...] = x + 0.5 * g_ref[0] * acc_ref[...]


def ffn_call(x, shift, scale, gate, w_in, w_out, pre=None, tm=FFN_ROW_TILE):
    n, d = x.shape
    tm = min(tm, n)
    groups = shift.shape[0]
    tiles_per_group = n // tm // groups
    d_ff = w_out.shape[0]
    tf = FF_CHUNK if d_ff % FF_CHUNK == 0 else d_ff

    def row_spec(width):
        return pl.BlockSpec((tm, width), lambda i: (i, 0))

    def mod_spec(m):
        return pl.BlockSpec((1,) + m.shape[1:], lambda i: (i // tiles_per_group, 0, 0))

    def whole(w):
        return pl.BlockSpec(w.shape, lambda i: (0, 0), pipeline_mode=pl.Buffered(1))

    args, specs = [x], [row_spec(d)]
    if pre is not None:
        a, w_pre, g_pre = pre
        args += [a, w_pre, g_pre]
        specs += [row_spec(a.shape[1]), whole(w_pre), mod_spec(g_pre)]
    args += [shift, scale, gate, w_in, w_out]
    specs += [mod_spec(shift), mod_spec(scale), mod_spec(gate), whole(w_in), whole(w_out)]
    return pl.pallas_call(
        functools.partial(_ffn_kernel, has_pre=pre is not None, d_ff=d_ff, tf=tf),
        grid=(n // tm,),
        in_specs=specs,
        out_specs=row_spec(d),
        out_shape=jax.ShapeDtypeStruct((n, d), F32),
        scratch_shapes=[pltpu.VMEM((tm, d), F32)],
        compiler_params=_params("arbitrary"),
        name="ffn_pre" if pre is not None else "ffn",
    )(*args)


def _head_mean_square(y, g_ref):
    y2 = (y * y).astype(BF16)
    w = g_ref.shape[0]
    parts = [_dot(y2[:, j * w:(j + 1) * w], g_ref[...]) for j in range(y.shape[1] // w)]
    return parts[0] if len(parts) == 1 else jnp.concatenate(parts, axis=1)


def _project(x, sh, sc, wq_ref, wk_ref, wv_ref, wqi_ref, wkw_ref, gq_ref, gk_ref, g_ref,
             qk_scale):
    hb = _modulate(x, sh, sc).astype(BF16)
    q = _dot(hb, wq_ref[...])
    qn = q * lax.rsqrt(_head_mean_square(q, g_ref) + RMS_EPS) * gq_ref[...] * qk_scale
    k = _dot(hb, wk_ref[...])
    kn = k * lax.rsqrt(_head_mean_square(k, g_ref) + RMS_EPS) * gk_ref[...]
    v = _dot(hb, wv_ref[...])
    qi = _dot(hb, wqi_ref[...])
    kw = _dot(hb, wkw_ref[...])
    return qn, kn, v, qi, kw


def _proj_kernel(x_ref, sh_ref, sc_ref, wq_ref, wk_ref, wv_ref, wqi_ref, wkw_ref, gq_ref,
                 gk_ref, g_ref,
                 qt_ref, k_ref, kb_ref, v_ref, vt_ref, qit_ref, ki_ref, kib_ref, wit_ref,
                 *, idx_dim, idx_heads, head_dim, v_rows, qk_scale, wi_scale):
    qn, kn, v, qi, kw = _project(x_ref[0], sh_ref[0], sc_ref[0], wq_ref, wk_ref, wv_ref,
                                 wqi_ref, wkw_ref, gq_ref, gk_ref, g_ref, qk_scale)
    tm = x_ref.shape[1]
    qt_ref[0, 0] = qn.T.astype(BF16)
    k_ref[0] = kn
    kb_ref[0] = kn.astype(BF16)
    v_ref[0] = v
    vt = v.T.astype(BF16)
    ones_rows = (lax.broadcasted_iota(I32, (v_rows - head_dim, tm), 0) == 0).astype(BF16)
    for g in range(v.shape[1] // head_dim):
        vt_ref[0, 0, g * v_rows:g * v_rows + head_dim, :] = vt[g * head_dim:(g + 1) * head_dim]
        vt_ref[0, 0, g * v_rows + head_dim:(g + 1) * v_rows, :] = ones_rows
    qit_ref[0, 0] = qi.T.astype(BF16)
    ki = kw[:, :idx_dim]
    ki_ref[0] = ki
    kib_ref[0] = ki.astype(BF16)
    wit_ref[0, 0] = kw.T[idx_dim:idx_dim + idx_heads, :] * wi_scale


def _sample_proj_kernel(x_ref, sh_ref, sc_ref, wq_ref, wk_ref, wv_ref, wqi_ref, wkw_ref,
                        gq_ref, gk_ref, g_ref,
                        q_ref, k_ref, v_ref, qi_ref, ki_ref, wi_ref,
                        *, idx_dim, idx_heads, qk_scale, wi_scale):
    qn, kn, v, qi, kw = _project(x_ref[...], sh_ref[0], sc_ref[0], wq_ref, wk_ref, wv_ref,
                                 wqi_ref, wkw_ref, gq_ref, gk_ref, g_ref, qk_scale)
    q_ref[...] = qn.astype(BF16)
    k_ref[...] = kn
    v_ref[...] = v
    qi_ref[...] = qi.astype(BF16)
    ki_ref[...] = kw[:, :idx_dim]
    wi_ref[...] = kw[:, idx_dim:idx_dim + idx_heads] * wi_scale


def _proj_weight_specs(ws, n_grid):
    zeros = (lambda *idx: (0, 0))
    return [pl.BlockSpec(w.shape, zeros, pipeline_mode=pl.Buffered(1)) for w in ws]


def attn_proj_prompt(x3, shift, scale, aw, dims, tm=ROW_TILE):
    b, t, d = x3.shape
    nt = t // tm
    hd, kvd, ihd, di, ih = dims["hd"], dims["kvd"], dims["ihd"], dims["idx_dim"], dims["idx_heads"]
    ws = [aw["wq"], aw["wk"], aw["wv"], aw["wqi"], aw["wkw"], aw["gq"], aw["gk"], aw["g"]]
    mod_spec = pl.BlockSpec((1, 1, d), lambda bi, i: (bi, 0, 0))
    nat = lambda w: pl.BlockSpec((1, tm, w), lambda bi, i: (bi, i, 0))
    tr = lambda r: pl.BlockSpec((1, 1, r, tm), lambda bi, i: (bi, i, 0, 0))
    vtd = kvd // dims["head_dim"] * dims["v_rows"]
    return pl.pallas_call(
        functools.partial(_proj_kernel, idx_dim=di, idx_heads=ih, head_dim=dims["head_dim"],
                          v_rows=dims["v_rows"], qk_scale=dims["qk_scale"] * LOG2_E,
                          wi_scale=dims["wi_scale"]),
        grid=(b, nt),
        in_specs=[nat(d), mod_spec, mod_spec] + _proj_weight_specs(ws, 2),
        out_specs=[tr(hd), nat(kvd), nat(kvd), nat(kvd), tr(vtd), tr(ihd), nat(di), nat(di), tr(ih)],
        out_shape=[
            jax.ShapeDtypeStruct((b, nt, hd, tm), BF16),
            jax.ShapeDtypeStruct((b, t, kvd), F32),
            jax.ShapeDtypeStruct((b, t, kvd), BF16),
            jax.ShapeDtypeStruct((b, t, kvd), F32),
            jax.ShapeDtypeStruct((b, nt, vtd, tm), BF16),
            jax.ShapeDtypeStruct((b, nt, ihd, tm), BF16),
            jax.ShapeDtypeStruct((b, t, di), F32),
            jax.ShapeDtypeStruct((b, t, di), BF16),
            jax.ShapeDtypeStruct((b, nt, ih, tm), F32),
        ],
        compiler_params=_params("arbitrary", "arbitrary"),
        name="attn_proj",
    )(x3, shift, scale, *ws)


def attn_proj_sample(x, shift, scale, aw, dims):
    n, d = x.shape
    hd, kvd, ihd, di, ih = dims["hd"], dims["kvd"], dims["ihd"], dims["idx_dim"], dims["idx_heads"]
    ws = [aw["wq"], aw["wk"], aw["wv"], aw["wqi"], aw["wkw"], aw["gq"], aw["gk"], aw["g"]]
    full = lambda w: pl.BlockSpec((n, w), lambda i: (0, 0))
    mod_spec = pl.BlockSpec((1, n, d), lambda i: (0, 0, 0))
    return pl.pallas_call(
        functools.partial(_sample_proj_kernel, idx_dim=di, idx_heads=ih,
                          qk_scale=dims["qk_scale"], wi_scale=dims["wi_scale"]),
        grid=(1,),
        in_specs=[full(d), mod_spec, mod_spec] + _proj_weight_specs(ws, 1),
        out_specs=[full(hd), full(kvd), full(kvd), full(ihd), full(di), full(ih)],
        out_shape=[
            jax.ShapeDtypeStruct((n, hd), BF16),
            jax.ShapeDtypeStruct((n, kvd), F32),
            jax.ShapeDtypeStruct((n, kvd), F32),
            jax.ShapeDtypeStruct((n, ihd), BF16),
            jax.ShapeDtypeStruct((n, di), F32),
            jax.ShapeDtypeStruct((n, ih), F32),
        ],
        compiler_params=_params("arbitrary"),
        name="attn_proj_sample",
    )(x, shift, scale, *ws)


def _dsa_kernel(qt_ref, qit_ref, wit_ref, k_ref, vt_ref, ki_ref, o_ref,
                keys_ref, hi_ref, lo_ref, qpad_ref, s_ref, acc_ref, m_ref, mnew_ref, l_ref, jlim_ref,
                *, n_heads, group, head_dim, v_rows, idx_heads, idx_dim, topk, seq_bits):
    tq = qt_ref.shape[-1]
    tk = tq
    tk2 = SELECT_CHUNKS * tk
    i = pl.program_id(1)
    n_chunks = i + 1
    n_pairs = (n_chunks + SELECT_CHUNKS - 1) // SELECT_CHUNKS
    q_idx = i * tq + lax.broadcasted_iota(I32, (tk, tq), 1)
    row_iota = lax.broadcasted_iota(I32, (tk, tq), 0)
    q_idx2 = i * tq + lax.broadcasted_iota(I32, (tk2, tq), 1)
    row_iota2 = lax.broadcasted_iota(I32, (tk2, tq), 0)

    def chunk_rows(c):
        return pl.ds(pl.multiple_of(c * tk, tk), tk)

    def pair_rows(j):
        return pl.ds(pl.multiple_of(j * tk2, tk2), tk2)

    def score_chunk(c, carry):
        kic = ki_ref[0, chunk_rows(c), :]
        acc = jnp.zeros((tk, tq), F32)
        for h in range(idx_heads):
            s = _dot(kic, qit_ref[0, 0, h * idx_dim:(h + 1) * idx_dim, :])
            acc = acc + jnp.maximum(s, 0.0) * wit_ref[0, 0, h:h + 1, :]
        acc = jnp.where(c * tk + row_iota <= q_idx, acc, NEG_INF)
        key = _float_key(acc)
        keys_ref[chunk_rows(c), :] = key
        hi_ref[chunk_rows(c), :] = lax.shift_right_arithmetic(key, 16).astype(I16)
        lo_ref[chunk_rows(c), :] = ((key & 0xFFFF) - HALF16).astype(I16)
        return carry

    lax.fori_loop(0, n_chunks, score_chunk, 0)

    def pad_chunk(c, carry):
        keys_ref[chunk_rows(c), :] = jnp.full((tk, tq), INT_MIN, I32)
        hi_ref[chunk_rows(c), :] = jnp.full((tk, tq), INT16_MIN, I16)
        lo_ref[chunk_rows(c), :] = jnp.full((tk, tq), INT16_MIN, I16)
        return carry

    lax.fori_loop(n_chunks, n_pairs * SELECT_CHUNKS, pad_chunk, 0)

    def count(pred):
        def body(j, cnt):
            kidx = j * tk2 + row_iota2
            return cnt + _fold_rows(pred(keys_ref[pair_rows(j), :], kidx).astype(I32), jnp.add)
        cnt8 = lax.fori_loop(0, n_pairs, body, jnp.zeros((SUBLANES, tq), I32))
        return jnp.sum(cnt8, axis=0, keepdims=True)

    def count_ge16(ref, cand):
        cand16 = cand.astype(I16)
        def body(j, cnt):
            m = (ref[pair_rows(j), :] >= cand16).astype(I16)
            return cnt + _fold_rows(m, jnp.add, BF16_SUBLANES).astype(I32)
        cnt16 = lax.fori_loop(0, n_pairs, body, jnp.zeros((BF16_SUBLANES, tq), I32))
        return jnp.sum(cnt16, axis=0, keepdims=True)

    def bisect16(ref, k_need, n_all):
        zero = jnp.zeros((1, tq), I32)
        c0 = count_ge16(ref, zero)
        nonneg = c0 >= k_need
        init = (jnp.where(nonneg, zero, INT16_MIN), jnp.where(nonneg, c0, n_all),
                jnp.where(nonneg, zero, c0))

        def search(it, carry):
            prefix, n_ge, n_up = carry
            cand = prefix | jnp.left_shift(jnp.int32(1), 14 - it)
            c = count_ge16(ref, cand)
            ok = c >= k_need
            return jnp.where(ok, cand, prefix), jnp.where(ok, c, n_ge), jnp.where(ok, n_up, c)

        return lax.fori_loop(0, 15, search, init)

    zero = jnp.zeros((1, tq), I32)
    thr_hi, n_ge_hi, n_gt_hi = bisect16(hi_ref, topk, n_chunks * tk)

    thr_hi16 = thr_hi.astype(I16)

    def mask_low(j, carry):
        keep = hi_ref[pair_rows(j), :] == thr_hi16
        lo_ref[pair_rows(j), :] = jnp.where(keep, lo_ref[pair_rows(j), :], INT16_MIN).astype(I16)
        return carry

    lax.fori_loop(0, n_pairs, mask_low, 0)
    thr_lo, n_ge_lo, n_gt_lo = bisect16(lo_ref, topk - n_gt_hi, n_ge_hi - n_gt_hi)
    thr = lax.shift_left(thr_hi, 16) | (thr_lo + HALF16)
    n_ge = n_gt_hi + n_ge_lo
    n_gt = n_gt_hi + n_gt_lo
    need = topk - n_gt

    jlim_ref[...] = jnp.full((1, tq), (1 << seq_bits) - 1, I32)
    excess = jnp.logical_and(n_ge > topk, thr > KEY_NEG_INF)

    @pl.when(jnp.max(excess.astype(I32)) > 0)
    def _():
        def tie_search(it, j):
            cand = j | jnp.left_shift(jnp.int32(1), seq_bits - 1 - it)
            n_before = count(lambda kc, kidx: jnp.logical_and(kc == thr, kidx < cand))
            return jnp.where(n_before < need, cand, j)
        jlim_ref[...] = lax.fori_loop(0, seq_bits, tie_search, zero)

    jlim = jlim_ref[...]

    def bias_pair(j, carry):
        kc = keys_ref[pair_rows(j), :]
        kidx = j * tk2 + row_iota2
        sel = jnp.logical_or(kc > thr, jnp.logical_and(kc == thr, kidx <= jlim))
        sel = jnp.logical_and(sel, kidx <= q_idx2)
        bias = jnp.where(sel, 0.0, NEG_INF).astype(F32)
        keys_ref[pair_rows(j), :] = lax.bitcast_convert_type(bias, I32)
        return carry

    lax.fori_loop(0, n_pairs, bias_pair, 0)

    qpad_ref[...] = jnp.zeros(qpad_ref.shape, BF16)
    for h in range(n_heads):
        g = h // group
        qpad_ref[h, g * head_dim:(g + 1) * head_dim, :] = qt_ref[0, 0, h * head_dim:(h + 1) * head_dim, :]
    m_ref[...] = jnp.full(m_ref.shape, NEG_INF, F32)
    l_ref[...] = jnp.zeros(l_ref.shape, F32)
    acc_ref[...] = jnp.zeros(acc_ref.shape, F32)

    def logits_head(h, kc, bias, buf, m_prev):
        s = _dot(kc, qpad_ref[h]) + bias
        s_ref[buf, h] = s
        mnew_ref[buf, h:h + 1, :] = jnp.maximum(
            m_prev, jnp.max(_fold_rows(s, jnp.maximum), axis=0, keepdims=True))

    def weights_head(h, c, buf):
        g = h // group
        rows = slice(h * head_dim, (h + 1) * head_dim)
        m_old = m_ref[h:h + 1, :]
        m_new = mnew_ref[buf, h:h + 1, :]
        m_safe = jnp.where(m_new == NEG_INF, 0.0, m_new)
        p = jnp.exp2(s_ref[buf, h] - m_safe).astype(BF16)
        alpha = jnp.exp2(m_old - m_safe)
        pv = _dot(vt_ref[0, c, g * v_rows:(g + 1) * v_rows, :], p)
        acc_ref[rows, :] = alpha * acc_ref[rows, :] + pv[0:head_dim]
        l_ref[h:h + 1, :] = alpha * l_ref[h:h + 1, :] + pv[head_dim:head_dim + 1]
        m_ref[h:h + 1, :] = m_new

    def chunk_operands(c):
        return k_ref[0, chunk_rows(c), :], lax.bitcast_convert_type(keys_ref[chunk_rows(c), :], F32)

    kc0, bias0 = chunk_operands(0)
    for h in range(n_heads):
        logits_head(h, kc0, bias0, 0, m_ref[h:h + 1, :])

    def step(c, src, dst):
        kc, bias = chunk_operands(c + 1)
        for h in range(n_heads):
            logits_head(h, kc, bias, dst, mnew_ref[src, h:h + 1, :])
            weights_head(h, c, src)

    def last_weights(buf):
        for h in range(n_heads):
            weights_head(h, n_chunks - 1, buf)

    def two_steps(j, carry):
        step(2 * j, 0, 1)
        step(2 * j + 1, 1, 0)
        return carry

    n_steps = n_chunks - 1
    lax.fori_loop(0, n_steps // 2, two_steps, 0)

    @pl.when(n_steps % 2 == 1)
    def _():
        step(n_steps - 1, 0, 1)
        last_weights(1)

    @pl.when(n_steps % 2 == 0)
    def _():
        last_weights(0)

    for h in range(n_heads):
        rows = slice(h * head_dim, (h + 1) * head_dim)
        acc_ref[rows, :] = acc_ref[rows, :] / l_ref[h:h + 1, :]
    o_ref[0] = acc_ref[...].T.astype(BF16)


def dsa_prompt(qt, qit, wit, kb, vt, kib, dims):
    b, nt, hd, tq = qt.shape
    t = nt * tq
    kvd, ihd, ih = kb.shape[-1], qit.shape[2], wit.shape[2]
    nh = dims["n_heads"]
    topk = min(TOPK_MAX, t // 4)
    key_rows = -(-nt // SELECT_CHUNKS) * SELECT_CHUNKS * tq
    blk = lambda r: pl.BlockSpec((1, 1, r, tq), lambda bi, i: (bi, i, 0, 0))
    per_batch3 = lambda w: pl.BlockSpec((1, t, w), lambda bi, i: (bi, 0, 0),
                                        pipeline_mode=pl.Buffered(1))
    return pl.pallas_call(
        functools.partial(_dsa_kernel, n_heads=nh, group=dims["group"],
                          head_dim=dims["head_dim"], v_rows=dims["v_rows"], idx_heads=ih,
                          idx_dim=dims["idx_dim"], topk=topk, seq_bits=max(1, (t - 1).bit_length())),
        grid=(b, nt),
        in_specs=[blk(hd), blk(ihd), blk(ih), per_batch3(kvd),
                  pl.BlockSpec((1, nt) + vt.shape[2:], lambda bi, i: (bi, 0, 0, 0),
                               pipeline_mode=pl.Buffered(1)),
                  per_batch3(dims["idx_dim"])],
        out_specs=pl.BlockSpec((1, tq, hd), lambda bi, i: (bi, i, 0)),
        out_shape=jax.ShapeDtypeStruct((b, t, hd), BF16),
        scratch_shapes=[pltpu.VMEM((key_rows, tq), I32), pltpu.VMEM((key_rows, tq), I16),
                        pltpu.VMEM((key_rows, tq), I16), pltpu.VMEM((nh, kvd, tq), BF16),
                        pltpu.VMEM((2, nh, tq, tq), F32), pltpu.VMEM((hd, tq), F32),
                        pltpu.VMEM((nh, tq), F32), pltpu.VMEM((2, nh, tq), F32),
                        pltpu.VMEM((nh, tq), F32), pltpu.VMEM((1, tq), I32)],
        compiler_params=_params("arbitrary", "arbitrary"),
        name="dsa_prompt",
    )(qt, qit, wit, kb, vt, kib)


def _dsa_sample_kernel(pt_ref, qpad_ref, hsel_ref, qi_ref, wi_ref, knew_ref, vnew_ref, kinew_ref,
                       ckt_ref, cvt_ref, ckit_ref, o_ref,
                       ktbuf, vtbuf, kitbuf, dense_ref, sem,
                       *, layer, n_pages, page, topk, total_bits, n_groups, head_dim, key_chunk):
    bi = pl.program_id(0)
    n_seq = pl.num_programs(0)
    past = n_pages * page
    slot = bi % 2

    def page_copies(seq, buf, p):
        phys = pt_ref[seq, p]
        cols = pl.ds(pl.multiple_of(p * page, page), page)
        return (pltpu.make_async_copy(ckt_ref.at[layer, phys], ktbuf.at[buf, :, cols], sem.at[buf, 0]),
                pltpu.make_async_copy(cvt_ref.at[layer, phys], vtbuf.at[buf, :, cols], sem.at[buf, 1]),
                pltpu.make_async_copy(ckit_ref.at[layer, phys], kitbuf.at[buf, :, cols], sem.at[buf, 2]))

    def start_seq(seq, buf):
        def body(p, carry):
            for cp in page_copies(seq, buf, p):
                cp.start()
            return carry
        lax.fori_loop(0, n_pages, body, 0)

    def wait_seq(seq, buf):
        def body(p, carry):
            for cp in page_copies(seq, buf, p):
                cp.wait()
            return carry
        lax.fori_loop(0, n_pages, body, 0)

    @pl.when(bi == 0)
    def _():
        start_seq(0, 0)

    @pl.when(bi + 1 < n_seq)
    def _():
        start_seq(bi + 1, 1 - slot)

    wait_seq(bi, slot)
    chunks = [slice(c * key_chunk, (c + 1) * key_chunk) for c in range(past // key_chunk)]

    qi = qi_ref[0]
    wi = wi_ref[0]
    score = jnp.concatenate(
        [jnp.sum(jnp.maximum(_dot(qi, kitbuf[slot, :, ch].astype(BF16)), 0.0) * wi, axis=0,
                 keepdims=True) for ch in chunks], axis=1) + 0.0
    s_new = jnp.sum(qi.astype(F32) * kinew_ref[0], axis=1, keepdims=True)
    score_new = jnp.sum(jnp.maximum(s_new, 0.0) * wi, axis=0, keepdims=True) + 0.0
    keys = _float_key(score)
    key_new = _float_key(score_new)
    kidx = lax.broadcasted_iota(I32, (1, past), 1)

    width = past // SUBLANES
    for r in range(SUBLANES):
        dense_ref[r:r + 1, :] = keys[:, r * width:(r + 1) * width]
    keys_d = dense_ref[...]
    kidx_d = (lax.broadcasted_iota(I32, (SUBLANES, width), 0) * width
              + lax.broadcasted_iota(I32, (SUBLANES, width), 1))
    idx_new = jnp.full((1, 1), past, I32)

    def count(pred):
        n = jnp.sum(pred(keys_d, kidx_d).astype(I32), axis=1, keepdims=True)
        return jnp.sum(n, axis=0, keepdims=True) + pred(key_new, idx_new).astype(I32)

    def digits(base, n_bits, holds):
        shift = n_bits
        while shift > 0:
            step = min(RADIX_BITS, shift)
            shift -= step
            digit = jnp.zeros((1, 1), I32)
            for j in range(1, 1 << step):
                digit = digit + holds(base | (j << shift)).astype(I32)
            base = base | lax.shift_left(digit, shift)
        return base

    zero = jnp.zeros((1, 1), I32)
    sign = jnp.where(count(lambda k, j: k >= zero) >= topk, zero, INT_MIN)
    thr = digits(sign, 31, lambda cand: count(lambda k, j: k >= cand) >= topk)
    need = topk - count(lambda k, j: k > thr)
    jlim = digits(zero, total_bits, lambda cand: count(
        lambda k, j: jnp.logical_and(k == thr, j < cand)) < need)

    def selected(k, j):
        return jnp.logical_or(k > thr, jnp.logical_and(k == thr, j <= jlim))

    sel = selected(keys, kidx)
    sel_new = selected(key_new, jnp.full((1, 1), past, I32))

    qpad = qpad_ref[0]
    logits = jnp.concatenate([_dot(qpad, ktbuf[slot, :, ch].astype(BF16)) for ch in chunks], axis=1)
    logits = jnp.where(sel, logits, NEG_INF)
    logit_new = jnp.sum(qpad.astype(F32) * knew_ref[0], axis=1, keepdims=True)
    logit_new = jnp.where(sel_new, logit_new, NEG_INF)
    m = jnp.maximum(jnp.max(logits, axis=1, keepdims=True), logit_new)
    p = jnp.exp(logits - m)
    p_new = jnp.exp(logit_new - m)
    denom = jnp.sum(p, axis=1, keepdims=True) + p_new
    pb = p.astype(BF16)
    o_all = p_new * vnew_ref[0]
    for ch in chunks:
        o_all = o_all + _dot_nt(pb[:, ch], vtbuf[slot, :, ch].astype(BF16))
    o_all = o_all / denom
    out = jnp.zeros((o_all.shape[0], head_dim), F32)
    for g in range(n_groups):
        out = out + hsel_ref[g] * o_all[:, g * head_dim:(g + 1) * head_dim]
    o_ref[0] = out.astype(BF16)


def dsa_sample(q, k_new, v_new, qi, ki_new, wi, cache_kt, cache_vt, cache_kit, layer, page_table, dims):
    db, hd = q.shape
    n_layers, n_pool, kvd, page = cache_kt.shape
    dh = dims["head_dim"]
    kvh = kvd // dh
    n_pages = page_table.shape[1]
    past = n_pages * page
    nh, group, ih, di = dims["n_heads"], dims["group"], dims["idx_heads"], dims["idx_dim"]
    topk = min(TOPK_MAX, (past + 1) // 4)
    key_chunk = SAMPLE_KEY_CHUNK if past % SAMPLE_KEY_CHUNK == 0 else past
    head_group = jnp.arange(nh) // group
    onehot = (head_group[:, None] == jnp.arange(kvh)[None, :])
    qpad = (q.reshape(db, nh, 1, dh) * onehot[None, :, :, None].astype(q.dtype)).reshape(db, nh, kvd)
    hsel = jnp.transpose(onehot.astype(F32))[:, :, None]
    row = lambda w: pl.BlockSpec((1, 1, w), lambda bi, pt: (bi, 0, 0))
    any_spec = pl.BlockSpec(memory_space=pl.ANY)
    grid_spec = pltpu.PrefetchScalarGridSpec(
        num_scalar_prefetch=1,
        grid=(db,),
        in_specs=[
            pl.BlockSpec((1, nh, kvd), lambda bi, pt: (bi, 0, 0)),
            pl.BlockSpec((kvh, nh, 1), lambda bi, pt: (0, 0, 0)),
            pl.BlockSpec((1, ih, di), lambda bi, pt: (bi, 0, 0)),
            pl.BlockSpec((1, ih, 1), lambda bi, pt: (bi, 0, 0)),
            row(kvd), row(kvd), row(di),
            any_spec, any_spec, any_spec,
        ],
        out_specs=pl.BlockSpec((1, nh, dh), lambda bi, pt: (bi, 0, 0)),
        scratch_shapes=[pltpu.VMEM((2, kvd, past), F32), pltpu.VMEM((2, kvd, past), F32),
                        pltpu.VMEM((2, di, past), F32), pltpu.VMEM((SUBLANES, past // SUBLANES), I32),
                        pltpu.SemaphoreType.DMA((2, 3))],
    )
    out = pl.pallas_call(
        functools.partial(_dsa_sample_kernel, layer=layer, n_pages=n_pages, page=page, topk=topk,
                          total_bits=max(1, past.bit_length()), n_groups=kvh, head_dim=dh,
                          key_chunk=key_chunk),
        grid_spec=grid_spec,
        out_shape=jax.ShapeDtypeStruct((db, nh, dh), BF16),
        compiler_params=_params("arbitrary"),
        name="dsa_sample",
    )(page_table, qpad, hsel, qi.reshape(db, ih, di), wi.reshape(db, ih, 1),
      k_new.reshape(db, 1, kvd), v_new.reshape(db, 1, kvd), ki_new.reshape(db, 1, di),
      cache_kt, cache_vt, cache_kit)
    return out.reshape(db, hd)


def _conv_prompt_kernel(x_ref, sh_ref, sc_ref, win_ref, cw_ref, a_ref, st_ref, carry_ref):
    i = pl.program_id(1)
    d = x_ref.shape[-1]
    tm = x_ref.shape[1]
    hb = _modulate(x_ref[0], sh_ref[0], sc_ref[0]).astype(BF16)
    b_gate = _dot(hb, win_ref[:, 0:d])
    z = _dot(hb, win_ref[:, d:2 * d]) * _dot(hb, win_ref[:, 2 * d:3 * d])

    @pl.when(i == 0)
    def _():
        carry_ref[...] = jnp.zeros_like(carry_ref)

    row = lax.broadcasted_iota(I32, (tm, d), 0)
    prev1 = carry_ref[SUBLANES - 1:SUBLANES, :]
    prev2 = carry_ref[SUBLANES - 2:SUBLANES - 1, :]
    z1 = jnp.where(row == 0, prev1, pltpu.roll(z, 1, axis=0))
    z2 = jnp.where(row == 0, prev2, jnp.where(row == 1, prev1, pltpu.roll(z, 2, axis=0)))
    y = cw_ref[0:1, :] * z2 + cw_ref[1:2, :] * z1 + cw_ref[2:3, :] * z
    a_ref[0] = (b_gate * y).astype(BF16)
    tail = z[tm - SUBLANES:tm, :]
    carry_ref[...] = tail
    st_ref[0] = tail


def conv_prompt(x3, shift, scale, w_in, conv_w8, tm=ROW_TILE):
    b, t, d = x3.shape
    mod_spec = pl.BlockSpec((1, 1, d), lambda bi, i: (bi, 0, 0))
    return pl.pallas_call(
        _conv_prompt_kernel,
        grid=(b, t // tm),
        in_specs=[pl.BlockSpec((1, tm, d), lambda bi, i: (bi, i, 0)), mod_spec, mod_spec,
                  pl.BlockSpec(w_in.shape, lambda bi, i: (0, 0), pipeline_mode=pl.Buffered(1)),
                  pl.BlockSpec(conv_w8.shape, lambda bi, i: (0, 0))],
        out_specs=[pl.BlockSpec((1, tm, d), lambda bi, i: (bi, i, 0)),
                   pl.BlockSpec((1, SUBLANES, d), lambda bi, i: (bi, 0, 0))],
        out_shape=[jax.ShapeDtypeStruct((b, t, d), BF16),
                   jax.ShapeDtypeStruct((b, SUBLANES, d), F32)],
        scratch_shapes=[pltpu.VMEM((SUBLANES, d), F32)],
        compiler_params=_params("arbitrary", "arbitrary"),
        name="conv_prompt",
    )(x3, shift, scale, w_in, conv_w8)


def _conv_sample_kernel(x_ref, sh_ref, sc_ref, win_ref, cw_ref, p0_ref, p1_ref, a_ref, z_ref):
    d = x_ref.shape[-1]
    hb = _modulate(x_ref[...], sh_ref[0], sc_ref[0]).astype(BF16)
    b_gate = _dot(hb, win_ref[:, 0:d])
    z = _dot(hb, win_ref[:, d:2 * d]) * _dot(hb, win_ref[:, 2 * d:3 * d])
    y = cw_ref[0:1, :] * p0_ref[...] + cw_ref[1:2, :] * p1_ref[...] + cw_ref[2:3, :] * z
    a_ref[...] = (b_gate * y).astype(BF16)
    z_ref[...] = z


def conv_sample(x, shift, scale, w_in, conv_w8, prefix0, prefix1):
    n, d = x.shape
    full = pl.BlockSpec((n, d), lambda i: (0, 0))
    mod_spec = pl.BlockSpec((1, n, d), lambda i: (0, 0, 0))
    return pl.pallas_call(
        _conv_sample_kernel,
        grid=(1,),
        in_specs=[full, mod_spec, mod_spec,
                  pl.BlockSpec(w_in.shape, lambda i: (0, 0), pipeline_mode=pl.Buffered(1)),
                  pl.BlockSpec(conv_w8.shape, lambda i: (0, 0)), full, full],
        out_specs=[full, full],
        out_shape=[jax.ShapeDtypeStruct((n, d), BF16), jax.ShapeDtypeStruct((n, d), F32)],
        compiler_params=_params("arbitrary"),
        name="conv_sample",
    )(x, shift, scale, w_in, conv_w8, prefix0, prefix1)


def kernel(x_prompt, x_sample, cache_k, cache_v, cache_kidx, state_conv, page_table, c_prompt, c_sample, w_ada, b_ada, w_ffn_in, w_ffn_out, w_attn_in, w_attn_out, q_norm_gain, k_norm_gain, w_conv_in, conv_w, w_conv_out):
    b, t, d = x_prompt.shape
    db, dt, _ = x_sample.shape
    assert dt == 1, "the sample path handles one new token per sequence"
    depth = w_ada.shape[0]
    head_dim = q_norm_gain.shape[-1]
    n_heads = w_attn_out.shape[1] // head_dim
    kvh = cache_k.shape[3]
    idx_dim = cache_kidx.shape[-1]
    hd, kvd = n_heads * head_dim, kvh * head_dim
    idx_heads = (w_attn_in.shape[-1] - hd - 2 * kvd - idx_dim) // (idx_dim + 1)
    ihd = idx_heads * idx_dim
    conv_width = conv_w.shape[1]
    assert conv_width == 3
    dims = dict(n_heads=n_heads, group=n_heads // kvh, head_dim=head_dim, hd=hd, kvd=kvd, ihd=ihd,
                idx_dim=idx_dim, idx_heads=idx_heads, v_rows=head_dim + BF16_SUBLANES,
                qk_scale=head_dim ** -0.5,
                wi_scale=idx_heads ** -0.5 * idx_dim ** -0.5)

    n_c = b + db
    n_c_pad = -(-n_c // SUBLANES) * SUBLANES
    c_all = jnp.concatenate([c_prompt, c_sample, jnp.zeros((n_c_pad - n_c, d), F32)], axis=0)
    mods = ada_mods(c_all, w_ada, b_ada).reshape(depth, n_c_pad, N_MOD, d)

    def prompt_mod(layer, j):
        return mods[layer, :b, j][:, None, :]

    def sample_mod(layer, j):
        return mods[layer, b:n_c, j][None]

    gw = 256 if kvd % 256 == 0 and hd % 256 == 0 else kvd
    gi = jnp.arange(gw) // head_dim
    g_mat = jnp.where(gi[:, None] == gi[None, :], 1.0 / head_dim, 0.0).astype(BF16)

    n_attn, n_pool, page = cache_k.shape[:3]
    cache_kt = jnp.transpose(cache_k, (0, 1, 3, 4, 2)).reshape(n_attn, n_pool, kvd, page)
    cache_vt = jnp.transpose(cache_v, (0, 1, 3, 4, 2)).reshape(n_attn, n_pool, kvd, page)
    cache_kit = jnp.transpose(cache_kidx, (0, 1, 3, 2))

    xp = x_prompt.reshape(b * t, d)
    xs = x_sample.reshape(db, d)
    pk, pv, pki, pconv, sk, sv, ski, sconv = [], [], [], [], [], [], [], []
    n_mixers = 2
    for layer in range(depth):
        li = layer // n_mixers
        w1_in, w1_out = w_ffn_in[layer, 0].astype(BF16), w_ffn_out[layer, 0].astype(BF16)
        w2_in, w2_out = w_ffn_in[layer, 1].astype(BF16), w_ffn_out[layer, 1].astype(BF16)
        pm = [prompt_mod(layer, j) for j in range(N_MOD)]
        sm = [sample_mod(layer, j) for j in range(N_MOD)]
        xp = ffn_call(xp, pm[0], pm[1], pm[2], w1_in, w1_out)
        xs = ffn_call(xs, sm[0], sm[1], sm[2], w1_in, w1_out)
        if layer % n_mixers == 0:
            wa = w_attn_in[li].astype(BF16)
            o1, o2, o3, o4, o5 = hd, hd + kvd, hd + 2 * kvd, hd + 2 * kvd + ihd, hd + 2 * kvd + ihd + idx_dim
            wkw = jnp.pad(wa[:, o4:], ((0, 0), (0, 128 - (idx_dim + idx_heads))))
            aw = dict(wq=wa[:, :o1], wk=wa[:, o1:o2], wv=wa[:, o2:o3], wqi=wa[:, o3:o4], wkw=wkw,
                      gq=jnp.tile(q_norm_gain[li], n_heads)[None, :],
                      gk=jnp.tile(k_norm_gain[li], kvh)[None, :], g=g_mat)
            w_mix_out = w_attn_out[li].astype(BF16)
            qt, k, kb, v, vt, qit, ki, kib, wit = attn_proj_prompt(xp.reshape(b, t, d), pm[3], pm[4], aw, dims)
            ap = dsa_prompt(qt, qit, wit, kb, vt, kib, dims).reshape(b * t, hd)
            pk.append(k.reshape(b, t, kvh, head_dim))
            pv.append(v.reshape(b, t, kvh, head_dim))
            pki.append(ki)
            qs, ks, vs, qis, kis, wis = attn_proj_sample(xs, sm[3], sm[4], aw, dims)
            a_s = dsa_sample(qs, ks, vs, qis, kis, wis, cache_kt, cache_vt, cache_kit, li,
                             page_table, dims)
            sk.append(ks.reshape(db, 1, kvh, head_dim))
            sv.append(vs.reshape(db, 1, kvh, head_dim))
            ski.append(kis.reshape(db, 1, idx_dim))
        else:
            wc_in = w_conv_in[li].astype(BF16)
            w_mix_out = w_conv_out[li].astype(BF16)
            cw8 = jnp.pad(conv_w[li], ((0, SUBLANES - conv_width), (0, 0)))
            ap, st = conv_prompt(xp.reshape(b, t, d), pm[3], pm[4], wc_in, cw8)
            ap = ap.reshape(b * t, d)
            pconv.append(st[:, SUBLANES - (conv_width - 1):, :])
            prefix = state_conv[li]
            a_s, zs = conv_sample(xs, sm[3], sm[4], wc_in, cw8, prefix[:, 0], prefix[:, 1])
            sconv.append(jnp.stack([prefix[:, 1], zs], axis=1))
        xp = ffn_call(xp, pm[6], pm[7], pm[8], w2_in, w2_out, pre=(ap, w_mix_out, pm[5]))
        xs = ffn_call(xs, sm[6], sm[7], sm[8], w2_in, w2_out, pre=(a_s, w_mix_out, sm[5]))

    return (xp.reshape(b, t, d), xs.reshape(db, 1, d),
            jnp.stack(pk), jnp.stack(pv), jnp.stack(pki), jnp.stack(pconv),
            jnp.stack(sk), jnp.stack(sv), jnp.stack(ski), jnp.stack(sconv))
```

```python
import functools

import jax
import jax.numpy as jnp
from jax import lax
from jax.experimental import pallas as pl
from jax.experimental.pallas import tpu as pltpu

F32 = jnp.float32
BF16 = jnp.bfloat16
I32 = jnp.int32
I16 = jnp.int16

RMS_EPS = 1e-6
TOPK_MAX = 256
N_MOD = 9
SUBLANES = 8
BF16_SUBLANES = 16
LOG2_E = 1.4426950408889634
NEG_INF = float("-inf")
INT_MIN = -(2 ** 31)
INT16_MIN = -(2 ** 15)
HALF16 = 2 ** 15
KEY_NEG_INF = (0xFF800000 ^ 0x7FFFFFFF) - (1 << 32)

VMEM_LIMIT_BYTES = 56 * 1024 * 1024
ROW_TILE = 256
FFN_ROW_TILE = 512
FF_CHUNK = 256
SAMPLE_KEY_CHUNK = 2048
SELECT_CHUNKS = 2
RADIX_BITS = 4


def _params(*sem):
    return pltpu.CompilerParams(dimension_semantics=sem, vmem_limit_bytes=VMEM_LIMIT_BYTES)


def _sigmoid(x):
    return 1.0 / (1.0 + jnp.exp(-x))


def _modulate(x, shift, scale):
    ms = jnp.mean(x * x, axis=-1, keepdims=True)
    return (x * lax.rsqrt(ms + RMS_EPS)) * (1.0 + scale) + shift


def _dot(a, b):
    return jnp.dot(a, b, preferred_element_type=F32)


def _dot_nt(a, b):
    return lax.dot_general(a, b, (((1,), (1,)), ((), ())), preferred_element_type=F32)


def _float_key(x):
    bits = lax.bitcast_convert_type(x, I32)
    return bits ^ (lax.shift_right_arithmetic(bits, 31) & 0x7FFFFFFF)


def _fold_rows(x, op, rows=SUBLANES):
    parts = [x[j * rows:(j + 1) * rows] for j in range(x.shape[0] // rows)]
    while len(parts) > 1:
        parts = [op(parts[j], parts[j + 1]) if j + 1 < len(parts) else parts[j]
                 for j in range(0, len(parts), 2)]
    return parts[0]


def _ada_kernel(c_ref, w_ref, b_ref, o_ref):
    c = c_ref[...]
    s = (c * _sigmoid(c)).astype(BF16)
    o_ref[0] = _dot(s, w_ref[0].astype(BF16)) + b_ref[0]


def ada_mods(c_all, w_ada, b_ada):
    depth, d, nd = w_ada.shape
    r = c_all.shape[0]
    return pl.pallas_call(
        _ada_kernel,
        grid=(depth, nd // d),
        in_specs=[
            pl.BlockSpec((r, d), lambda l, j: (0, 0)),
            pl.BlockSpec((1, d, d), lambda l, j: (l, 0, j)),
            pl.BlockSpec((1, 1, d), lambda l, j: (l, 0, j)),
        ],
        out_specs=pl.BlockSpec((1, r, d), lambda l, j: (l, 0, j)),
        out_shape=jax.ShapeDtypeStruct((depth, r, nd), F32),
        compiler_params=_params("arbitrary", "arbitrary"),
        name="ada_mods",
    )(c_all, w_ada, b_ada.reshape(depth, 1, nd))


def _ffn_kernel(*refs, has_pre, d_ff, tf):
    if has_pre:
        (x_ref, a_ref, wpre_ref, gpre_ref, sh_ref, sc_ref, g_ref, win_ref, wout_ref,
         o_ref, acc_ref) = refs
    else:
        x_ref, sh_ref, sc_ref, g_ref, win_ref, wout_ref, o_ref, acc_ref = refs
    x = x_ref[...]
    if has_pre:
        x = x + gpre_ref[0] * _dot(a_ref[...], wpre_ref[...])
    hb = _modulate(x, sh_ref[0], sc_ref[0]).astype(BF16)
    for c in range(d_ff // tf):
        u = _dot(hb, win_ref[:, c * tf:(c + 1) * tf])
        g = _dot(hb, win_ref[:, d_ff + c * tf:d_ff + (c + 1) * tf])
        act = ((u * _sigmoid(u)) * g).astype(BF16)
        y = _dot(act, wout_ref[c * tf:(c + 1) * tf, :])
        if c == 0:
            acc_ref[...] = y
        else:
            acc_ref[...] += y
    o_ref[...] = x + 0.5 * g_ref[0] * acc_ref[...]


def ffn_call(x, shift, scale, gate, w_in, w_out, pre=None, tm=FFN_ROW_TILE):
    n, d = x.shape
    tm = min(tm, n)
    groups = shift.shape[0]
    tiles_per_group = n // tm // groups
    d_ff = w_out.shape[0]
    tf = FF_CHUNK if d_ff % FF_CHUNK == 0 else d_ff

    def row_spec(width):
        return pl.BlockSpec((tm, width), lambda i: (i, 0))

    def mod_spec(m):
        return pl.BlockSpec((1,) + m.shape[1:], lambda i: (i // tiles_per_group, 0, 0))

    def whole(w):
        return pl.BlockSpec(w.shape, lambda i: (0, 0), pipeline_mode=pl.Buffered(1))

    args, specs = [x], [row_spec(d)]
    if pre is not None:
        a, w_pre, g_pre = pre
        args += [a, w_pre, g_pre]
        specs += [row_spec(a.shape[1]), whole(w_pre), mod_spec(g_pre)]
    args += [shift, scale, gate, w_in, w_out]
    specs += [mod_spec(shift), mod_spec(scale), mod_spec(gate), whole(w_in), whole(w_out)]
    return pl.pallas_call(
        functools.partial(_ffn_kernel, has_pre=pre is not None, d_ff=d_ff, tf=tf),
        grid=(n // tm,),
        in_specs=specs,
        out_specs=row_spec(d),
        out_shape=jax.ShapeDtypeStruct((n, d), F32),
        scratch_shapes=[pltpu.VMEM((tm, d), F32)],
        compiler_params=_params("arbitrary"),
        name="ffn_pre" if pre is not None else "ffn",
    )(*args)


def _head_mean_square(y, g_ref):
    y2 = (y * y).astype(BF16)
    w = g_ref.shape[0]
    parts = [_dot(y2[:, j * w:(j + 1) * w], g_ref[...]) for j in range(y.shape[1] // w)]
    return parts[0] if len(parts) == 1 else jnp.concatenate(parts, axis=1)


def _project(x, sh, sc, wq_ref, wk_ref, wv_ref, wqi_ref, wkw_ref, gq_ref, gk_ref, g_ref,
             qk_scale):
    hb = _modulate(x, sh, sc).astype(BF16)
    q = _dot(hb, wq_ref[...])
    qn = q * lax.rsqrt(_head_mean_square(q, g_ref) + RMS_EPS) * gq_ref[...] * qk_scale
    k = _dot(hb, wk_ref[...])
    kn = k * lax.rsqrt(_head_mean_square(k, g_ref) + RMS_EPS) * gk_ref[...]
    v = _dot(hb, wv_ref[...])
    qi = _dot(hb, wqi_ref[...])
    kw = _dot(hb, wkw_ref[...])
    return qn, kn, v, qi, kw


def _proj_kernel(x_ref, sh_ref, sc_ref, wq_ref, wk_ref, wv_ref, wqi_ref, wkw_ref, gq_ref,
                 gk_ref, g_ref,
                 qt_ref, k_ref, kb_ref, v_ref, vt_ref, qit_ref, ki_ref, kib_ref, wit_ref,
                 *, idx_dim, idx_heads, head_dim, v_rows, qk_scale, wi_scale):
    qn, kn, v, qi, kw = _project(x_ref[0], sh_ref[0], sc_ref[0], wq_ref, wk_ref, wv_ref,
                                 wqi_ref, wkw_ref, gq_ref, gk_ref, g_ref, qk_scale)
    tm = x_ref.shape[1]
    qt_ref[0, 0] = qn.T.astype(BF16)
    k_ref[0] = kn
    kb_ref[0] = kn.astype(BF16)
    v_ref[0] = v
    vt = v.T.astype(BF16)
    ones_rows = (lax.broadcasted_iota(I32, (v_rows - head_dim, tm), 0) == 0).astype(BF16)
    for g in range(v.shape[1] // head_dim):
        vt_ref[0, 0, g * v_rows:g * v_rows + head_dim, :] = vt[g * head_dim:(g + 1) * head_dim]
        vt_ref[0, 0, g * v_rows + head_dim:(g + 1) * v_rows, :] = ones_rows
    qit_ref[0, 0] = qi.T.astype(BF16)
    ki = kw[:, :idx_dim]
    ki_ref[0] = ki
    kib_ref[0] = ki.astype(BF16)
    wit_ref[0, 0] = kw.T[idx_dim:idx_dim + idx_heads, :] * wi_scale


def _sample_proj_kernel(x_ref, sh_ref, sc_ref, wq_ref, wk_ref, wv_ref, wqi_ref, wkw_ref,
                        gq_ref, gk_ref, g_ref,
                        q_ref, k_ref, v_ref, qi_ref, ki_ref, wi_ref,
                        *, idx_dim, idx_heads, qk_scale, wi_scale):
    qn, kn, v, qi, kw = _project(x_ref[...], sh_ref[0], sc_ref[0], wq_ref, wk_ref, wv_ref,
                                 wqi_ref, wkw_ref, gq_ref, gk_ref, g_ref, qk_scale)
    q_ref[...] = qn.astype(BF16)
    k_ref[...] = kn
    v_ref[...] = v
    qi_ref[...] = qi.astype(BF16)
    ki_ref[...] = kw[:, :idx_dim]
    wi_ref[...] = kw[:, idx_dim:idx_dim + idx_heads] * wi_scale


def _proj_weight_specs(ws, n_grid):
    zeros = (lambda *idx: (0, 0))
    return [pl.BlockSpec(w.shape, zeros, pipeline_mode=pl.Buffered(1)) for w in ws]


def attn_proj_prompt(x3, shift, scale, aw, dims, tm=ROW_TILE):
    b, t, d = x3.shape
    nt = t // tm
    hd, kvd, ihd, di, ih = dims["hd"], dims["kvd"], dims["ihd"], dims["idx_dim"], dims["idx_heads"]
    ws = [aw["wq"], aw["wk"], aw["wv"], aw["wqi"], aw["wkw"], aw["gq"], aw["gk"], aw["g"]]
    mod_spec = pl.BlockSpec((1, 1, d), lambda bi, i: (bi, 0, 0))
    nat = lambda w: pl.BlockSpec((1, tm, w), lambda bi, i: (bi, i, 0))
    tr = lambda r: pl.BlockSpec((1, 1, r, tm), lambda bi, i: (bi, i, 0, 0))
    vtd = kvd // dims["head_dim"] * dims["v_rows"]
    return pl.pallas_call(
        functools.partial(_proj_kernel, idx_dim=di, idx_heads=ih, head_dim=dims["head_dim"],
                          v_rows=dims["v_rows"], qk_scale=dims["qk_scale"] * LOG2_E,
                          wi_scale=dims["wi_scale"]),
        grid=(b, nt),
        in_specs=[nat(d), mod_spec, mod_spec] + _proj_weight_specs(ws, 2),
        out_specs=[tr(hd), nat(kvd), nat(kvd), nat(kvd), tr(vtd), tr(ihd), nat(di), nat(di), tr(ih)],
        out_shape=[
            jax.ShapeDtypeStruct((b, nt, hd, tm), BF16),
            jax.ShapeDtypeStruct((b, t, kvd), F32),
            jax.ShapeDtypeStruct((b, t, kvd), BF16),
            jax.ShapeDtypeStruct((b, t, kvd), F32),
            jax.ShapeDtypeStruct((b, nt, vtd, tm), BF16),
            jax.ShapeDtypeStruct((b, nt, ihd, tm), BF16),
            jax.ShapeDtypeStruct((b, t, di), F32),
            jax.ShapeDtypeStruct((b, t, di), BF16),
            jax.ShapeDtypeStruct((b, nt, ih, tm), F32),
        ],
        compiler_params=_params("arbitrary", "arbitrary"),
        name="attn_proj",
    )(x3, shift, scale, *ws)


def attn_proj_sample(x, shift, scale, aw, dims):
    n, d = x.shape
    hd, kvd, ihd, di, ih = dims["hd"], dims["kvd"], dims["ihd"], dims["idx_dim"], dims["idx_heads"]
    ws = [aw["wq"], aw["wk"], aw["wv"], aw["wqi"], aw["wkw"], aw["gq"], aw["gk"], aw["g"]]
    full = lambda w: pl.BlockSpec((n, w), lambda i: (0, 0))
    mod_spec = pl.BlockSpec((1, n, d), lambda i: (0, 0, 0))
    return pl.pallas_call(
        functools.partial(_sample_proj_kernel, idx_dim=di, idx_heads=ih,
                          qk_scale=dims["qk_scale"], wi_scale=dims["wi_scale"]),
        grid=(1,),
        in_specs=[full(d), mod_spec, mod_spec] + _proj_weight_specs(ws, 1),
        out_specs=[full(hd), full(kvd), full(kvd), full(ihd), full(di), full(ih)],
        out_shape=[
            jax.ShapeDtypeStruct((n, hd), BF16),
            jax.ShapeDtypeStruct((n, kvd), F32),
            jax.ShapeDtypeStruct((n, kvd), F32),
            jax.ShapeDtypeStruct((n, ihd), BF16),
            jax.ShapeDtypeStruct((n, di), F32),
            jax.ShapeDtypeStruct((n, ih), F32),
        ],
        compiler_params=_params("arbitrary"),
        name="attn_proj_sample",
    )(x, shift, scale, *ws)


def _dsa_kernel(qt_ref, qit_ref, wit_ref, k_ref, vt_ref, ki_ref, o_ref,
                keys_ref, hi_ref, lo_ref, qpad_ref, s_ref, acc_ref, m_ref, mnew_ref, l_ref, jlim_ref,
                *, n_heads, group, head_dim, v_rows, idx_heads, idx_dim, topk, seq_bits):
    tq = qt_ref.shape[-1]
    tk = tq
    tk2 = SELECT_CHUNKS * tk
    i = pl.program_id(1)
    n_chunks = i + 1
    n_pairs = (n_chunks + SELECT_CHUNKS - 1) // SELECT_CHUNKS
    q_idx = i * tq + lax.broadcasted_iota(I32, (tk, tq), 1)
    row_iota = lax.broadcasted_iota(I32, (tk, tq), 0)
    q_idx2 = i * tq + lax.broadcasted_iota(I32, (tk2, tq), 1)
    row_iota2 = lax.broadcasted_iota(I32, (tk2, tq), 0)

    def chunk_rows(c):
        return pl.ds(pl.multiple_of(c * tk, tk), tk)

    def pair_rows(j):
        return pl.ds(pl.multiple_of(j * tk2, tk2), tk2)

    def score_chunk(c, carry):
        kic = ki_ref[0, chunk_rows(c), :]
        acc = jnp.zeros((tk, tq), F32)
        for h in range(idx_heads):
            s = _dot(kic, qit_ref[0, 0, h * idx_dim:(h + 1) * idx_dim, :])
            acc = acc + jnp.maximum(s, 0.0) * wit_ref[0, 0, h:h + 1, :]
        kidx = c * tk + row_iota
        key = _float_key(jnp.where(kidx <= q_idx, acc, NEG_INF))
        span = 1 << seq_bits
        key = jnp.where(key > 0, key + span, jnp.where(key == 0, (span - 1) - kidx, key))
        keys_ref[chunk_rows(c), :] = key
        hi_ref[chunk_rows(c), :] = lax.shift_right_arithmetic(key, 16).astype(I16)
        lo_ref[chunk_rows(c), :] = ((key & 0xFFFF) - HALF16).astype(I16)
        return carry

    lax.fori_loop(0, n_chunks, score_chunk, 0)

    def pad_chunk(c, carry):
        keys_ref[chunk_rows(c), :] = jnp.full((tk, tq), INT_MIN, I32)
        hi_ref[chunk_rows(c), :] = jnp.full((tk, tq), INT16_MIN, I16)
        lo_ref[chunk_rows(c), :] = jnp.full((tk, tq), INT16_MIN, I16)
        return carry

    lax.fori_loop(n_chunks, n_pairs * SELECT_CHUNKS, pad_chunk, 0)

    def count(pred):
        def body(j, cnt):
            kidx = j * tk2 + row_iota2
            return cnt + _fold_rows(pred(keys_ref[pair_rows(j), :], kidx).astype(I32), jnp.add)
        cnt8 = lax.fori_loop(0, n_pairs, body, jnp.zeros((SUBLANES, tq), I32))
        return jnp.sum(cnt8, axis=0, keepdims=True)

    def count_ge16(ref, cand):
        cand16 = cand.astype(I16)
        def body(j, cnt):
            m = (ref[pair_rows(j), :] >= cand16).astype(I16)
            return cnt + _fold_rows(m, jnp.add, BF16_SUBLANES).astype(I32)
        cnt16 = lax.fori_loop(0, n_pairs, body, jnp.zeros((BF16_SUBLANES, tq), I32))
        return jnp.sum(cnt16, axis=0, keepdims=True)

    def bisect16(ref, k_need, n_all):
        zero = jnp.zeros((1, tq), I32)
        c0 = count_ge16(ref, zero)
        nonneg = c0 >= k_need
        init = (jnp.where(nonneg, zero, INT16_MIN), jnp.where(nonneg, c0, n_all),
                jnp.where(nonneg, zero, c0))

        def search(it, carry):
            prefix, n_ge, n_up = carry
            cand = prefix | jnp.left_shift(jnp.int32(1), 14 - it)
            c = count_ge16(ref, cand)
            ok = c >= k_need
            return jnp.where(ok, cand, prefix), jnp.where(ok, c, n_ge), jnp.where(ok, n_up, c)

        return lax.fori_loop(0, 15, search, init)

    zero = jnp.zeros((1, tq), I32)
    thr_hi, n_ge_hi, n_gt_hi = bisect16(hi_ref, topk, n_chunks * tk)

    thr_hi16 = thr_hi.astype(I16)

    def mask_low(j, carry):
        keep = hi_ref[pair_rows(j), :] == thr_hi16
        lo_ref[pair_rows(j), :] = jnp.where(keep, lo_ref[pair_rows(j), :], INT16_MIN).astype(I16)
        return carry

    lax.fori_loop(0, n_pairs, mask_low, 0)
    thr_lo, n_ge_lo, n_gt_lo = bisect16(lo_ref, topk - n_gt_hi, n_ge_hi - n_gt_hi)
    thr = lax.shift_left(thr_hi, 16) | (thr_lo + HALF16)
    n_ge = n_gt_hi + n_ge_lo
    n_gt = n_gt_hi + n_gt_lo
    need = topk - n_gt

    jlim_ref[...] = jnp.full((1, tq), (1 << seq_bits) - 1, I32)
    excess = jnp.logical_and(n_ge > topk, thr > KEY_NEG_INF)

    @pl.when(jnp.max(excess.astype(I32)) > 0)
    def _():
        def tie_search(it, j):
            cand = j | jnp.left_shift(jnp.int32(1), seq_bits - 1 - it)
            n_before = count(lambda kc, kidx: jnp.logical_and(kc == thr, kidx < cand))
            return jnp.where(n_before < need, cand, j)
        jlim_ref[...] = lax.fori_loop(0, seq_bits, tie_search, zero)

    jlim = jlim_ref[...]

    def bias_pair(j, carry):
        kc = keys_ref[pair_rows(j), :]
        kidx = j * tk2 + row_iota2
        sel = jnp.logical_or(kc > thr, jnp.logical_and(kc == thr, kidx <= jlim))
        sel = jnp.logical_and(sel, kidx <= q_idx2)
        bias = jnp.where(sel, 0.0, NEG_INF).astype(F32)
        keys_ref[pair_rows(j), :] = lax.bitcast_convert_type(bias, I32)
        return carry

    lax.fori_loop(0, n_pairs, bias_pair, 0)

    qpad_ref[...] = jnp.zeros(qpad_ref.shape, BF16)
    for h in range(n_heads):
        g = h // group
        qpad_ref[h, g * head_dim:(g + 1) * head_dim, :] = qt_ref[0, 0, h * head_dim:(h + 1) * head_dim, :]
    m_ref[...] = jnp.full(m_ref.shape, NEG_INF, F32)
    l_ref[...] = jnp.zeros(l_ref.shape, F32)
    acc_ref[...] = jnp.zeros(acc_ref.shape, F32)

    def logits_head(h, kc, bias, buf, m_prev):
        s = _dot(kc, qpad_ref[h]) + bias
        s_ref[buf, h] = s
        mnew_ref[buf, h:h + 1, :] = jnp.maximum(
            m_prev, jnp.max(_fold_rows(s, jnp.maximum), axis=0, keepdims=True))

    def weights_head(h, c, buf):
        g = h // group
        rows = slice(h * head_dim, (h + 1) * head_dim)
        m_old = m_ref[h:h + 1, :]
        m_new = mnew_ref[buf, h:h + 1, :]
        m_safe = jnp.where(m_new == NEG_INF, 0.0, m_new)
        p = jnp.exp2(s_ref[buf, h] - m_safe).astype(BF16)
        alpha = jnp.exp2(m_old - m_safe)
        pv = _dot(vt_ref[0, c, g * v_rows:(g + 1) * v_rows, :], p)
        acc_ref[rows, :] = alpha * acc_ref[rows, :] + pv[0:head_dim]
        l_ref[h:h + 1, :] = alpha * l_ref[h:h + 1, :] + pv[head_dim:head_dim + 1]
        m_ref[h:h + 1, :] = m_new

    def chunk_operands(c):
        return k_ref[0, chunk_rows(c), :], lax.bitcast_convert_type(keys_ref[chunk_rows(c), :], F32)

    kc0, bias0 = chunk_operands(0)
    for h in range(n_heads):
        logits_head(h, kc0, bias0, 0, m_ref[h:h + 1, :])

    def step(c, src, dst):
        kc, bias = chunk_operands(c + 1)
        for h in range(n_heads):
            logits_head(h, kc, bias, dst, mnew_ref[src, h:h + 1, :])
            weights_head(h, c, src)

    def last_weights(buf):
        for h in range(n_heads):
            weights_head(h, n_chunks - 1, buf)

    def two_steps(j, carry):
        step(2 * j, 0, 1)
        step(2 * j + 1, 1, 0)
        return carry

    n_steps = n_chunks - 1
    lax.fori_loop(0, n_steps // 2, two_steps, 0)

    @pl.when(n_steps % 2 == 1)
    def _():
        step(n_steps - 1, 0, 1)
        last_weights(1)

    @pl.when(n_steps % 2 == 0)
    def _():
        last_weights(0)

    for h in range(n_heads):
        rows = slice(h * head_dim, (h + 1) * head_dim)
        acc_ref[rows, :] = acc_ref[rows, :] / l_ref[h:h + 1, :]
    o_ref[0] = acc_ref[...].T.astype(BF16)


def dsa_prompt(qt, qit, wit, kb, vt, kib, dims):
    b, nt, hd, tq = qt.shape
    t = nt * tq
    kvd, ihd, ih = kb.shape[-1], qit.shape[2], wit.shape[2]
    nh = dims["n_heads"]
    topk = min(TOPK_MAX, t // 4)
    key_rows = -(-nt // SELECT_CHUNKS) * SELECT_CHUNKS * tq
    blk = lambda r: pl.BlockSpec((1, 1, r, tq), lambda bi, i: (bi, i, 0, 0))
    per_batch3 = lambda w: pl.BlockSpec((1, t, w), lambda bi, i: (bi, 0, 0),
                                        pipeline_mode=pl.Buffered(1))
    return pl.pallas_call(
        functools.partial(_dsa_kernel, n_heads=nh, group=dims["group"],
                          head_dim=dims["head_dim"], v_rows=dims["v_rows"], idx_heads=ih,
                          idx_dim=dims["idx_dim"], topk=topk, seq_bits=max(1, (t - 1).bit_length())),
        grid=(b, nt),
        in_specs=[blk(hd), blk(ihd), blk(ih), per_batch3(kvd),
                  pl.BlockSpec((1, nt) + vt.shape[2:], lambda bi, i: (bi, 0, 0, 0),
                               pipeline_mode=pl.Buffered(1)),
                  per_batch3(dims["idx_dim"])],
        out_specs=pl.BlockSpec((1, tq, hd), lambda bi, i: (bi, i, 0)),
        out_shape=jax.ShapeDtypeStruct((b, t, hd), BF16),
        scratch_shapes=[pltpu.VMEM((key_rows, tq), I32), pltpu.VMEM((key_rows, tq), I16),
                        pltpu.VMEM((key_rows, tq), I16), pltpu.VMEM((nh, kvd, tq), BF16),
                        pltpu.VMEM((2, nh, tq, tq), F32), pltpu.VMEM((hd, tq), F32),
                        pltpu.VMEM((nh, tq), F32), pltpu.VMEM((2, nh, tq), F32),
                        pltpu.VMEM((nh, tq), F32), pltpu.VMEM((1, tq), I32)],
        compiler_params=_params("arbitrary", "arbitrary"),
        name="dsa_prompt",
    )(qt, qit, wit, kb, vt, kib)


def _dsa_sample_kernel(pt_ref, qpad_ref, hsel_ref, qi_ref, wi_ref, knew_ref, vnew_ref, kinew_ref,
                       ckt_ref, cvt_ref, ckit_ref, o_ref,
                       ktbuf, vtbuf, kitbuf, dense_ref, sem,
                       *, layer, n_pages, page, topk, total_bits, n_groups, head_dim, key_chunk):
    bi = pl.program_id(0)
    n_seq = pl.num_programs(0)
    past = n_pages * page
    slot = bi % 2

    def page_copies(seq, buf, p):
        phys = pt_ref[seq, p]
        cols = pl.ds(pl.multiple_of(p * page, page), page)
        return (pltpu.make_async_copy(ckt_ref.at[layer, phys], ktbuf.at[buf, :, cols], sem.at[buf, 0]),
                pltpu.make_async_copy(cvt_ref.at[layer, phys], vtbuf.at[buf, :, cols], sem.at[buf, 1]),
                pltpu.make_async_copy(ckit_ref.at[layer, phys], kitbuf.at[buf, :, cols], sem.at[buf, 2]))

    def start_seq(seq, buf):
        def body(p, carry):
            for cp in page_copies(seq, buf, p):
                cp.start()
            return carry
        lax.fori_loop(0, n_pages, body, 0)

    def wait_seq(seq, buf):
        def body(p, carry):
            for cp in page_copies(seq, buf, p):
                cp.wait()
            return carry
        lax.fori_loop(0, n_pages, body, 0)

    @pl.when(bi == 0)
    def _():
        start_seq(0, 0)

    @pl.when(bi + 1 < n_seq)
    def _():
        start_seq(bi + 1, 1 - slot)

    wait_seq(bi, slot)
    chunks = [slice(c * key_chunk, (c + 1) * key_chunk) for c in range(past // key_chunk)]

    qi = qi_ref[0]
    wi = wi_ref[0]
    score = jnp.concatenate(
        [jnp.sum(jnp.maximum(_dot(qi, kitbuf[slot, :, ch].astype(BF16)), 0.0) * wi, axis=0,
                 keepdims=True) for ch in chunks], axis=1) + 0.0
    s_new = jnp.sum(qi.astype(F32) * kinew_ref[0], axis=1, keepdims=True)
    score_new = jnp.sum(jnp.maximum(s_new, 0.0) * wi, axis=0, keepdims=True) + 0.0
    keys = _float_key(score)
    key_new = _float_key(score_new)
    kidx = lax.broadcasted_iota(I32, (1, past), 1)

    width = past // SUBLANES
    for r in range(SUBLANES):
        dense_ref[r:r + 1, :] = keys[:, r * width:(r + 1) * width]
    keys_d = dense_ref[...]
    kidx_d = (lax.broadcasted_iota(I32, (SUBLANES, width), 0) * width
              + lax.broadcasted_iota(I32, (SUBLANES, width), 1))
    idx_new = jnp.full((1, 1), past, I32)

    def count(pred):
        n = jnp.sum(pred(keys_d, kidx_d).astype(I32), axis=1, keepdims=True)
        return jnp.sum(n, axis=0, keepdims=True) + pred(key_new, idx_new).astype(I32)

    def digits(base, n_bits, holds):
        shift = n_bits
        while shift > 0:
            step = min(RADIX_BITS, shift)
            shift -= step
            digit = jnp.zeros((1, 1), I32)
            for j in range(1, 1 << step):
                digit = digit + holds(base | (j << shift)).astype(I32)
            base = base | lax.shift_left(digit, shift)
        return base

    zero = jnp.zeros((1, 1), I32)
    sign = jnp.where(count(lambda k, j: k >= zero) >= topk, zero, INT_MIN)
    thr = digits(sign, 31, lambda cand: count(lambda k, j: k >= cand) >= topk)
    need = topk - count(lambda k, j: k > thr)
    jlim = digits(zero, total_bits, lambda cand: count(
        lambda k, j: jnp.logical_and(k == thr, j < cand)) < need)

    def selected(k, j):
        return jnp.logical_or(k > thr, jnp.logical_and(k == thr, j <= jlim))

    sel = selected(keys, kidx)
    sel_new = selected(key_new, jnp.full((1, 1), past, I32))

    qpad = qpad_ref[0]
    logits = jnp.concatenate([_dot(qpad, ktbuf[slot, :, ch].astype(BF16)) for ch in chunks], axis=1)
    logits = jnp.where(sel, logits, NEG_INF)
    logit_new = jnp.sum(qpad.astype(F32) * knew_ref[0], axis=1, keepdims=True)
    logit_new = jnp.where(sel_new, logit_new, NEG_INF)
    m = jnp.maximum(jnp.max(logits, axis=1, keepdims=True), logit_new)
    p = jnp.exp(logits - m)
    p_new = jnp.exp(logit_new - m)
    denom = jnp.sum(p, axis=1, keepdims=True) + p_new
    pb = p.astype(BF16)
    o_all = p_new * vnew_ref[0]
    for ch in chunks:
        o_all = o_all + _dot_nt(pb[:, ch], vtbuf[slot, :, ch].astype(BF16))
    o_all = o_all / denom
    out = jnp.zeros((o_all.shape[0], head_dim), F32)
    for g in range(n_groups):
        out = out + hsel_ref[g] * o_all[:, g * head_dim:(g + 1) * head_dim]
    o_ref[0] = out.astype(BF16)


def dsa_sample(q, k_new, v_new, qi, ki_new, wi, cache_kt, cache_vt, cache_kit, layer, page_table, dims):
    db, hd = q.shape
    n_layers, n_pool, kvd, page = cache_kt.shape
    dh = dims["head_dim"]
    kvh = kvd // dh
    n_pages = page_table.shape[1]
    past = n_pages * page
    nh, group, ih, di = dims["n_heads"], dims["group"], dims["idx_heads"], dims["idx_dim"]
    topk = min(TOPK_MAX, (past + 1) // 4)
    key_chunk = SAMPLE_KEY_CHUNK if past % SAMPLE_KEY_CHUNK == 0 else past
    head_group = jnp.arange(nh) // group
    onehot = (head_group[:, None] == jnp.arange(kvh)[None, :])
    qpad = (q.reshape(db, nh, 1, dh) * onehot[None, :, :, None].astype(q.dtype)).reshape(db, nh, kvd)
    hsel = jnp.transpose(onehot.astype(F32))[:, :, None]
    row = lambda w: pl.BlockSpec((1, 1, w), lambda bi, pt: (bi, 0, 0))
    any_spec = pl.BlockSpec(memory_space=pl.ANY)
    grid_spec = pltpu.PrefetchScalarGridSpec(
        num_scalar_prefetch=1,
        grid=(db,),
        in_specs=[
            pl.BlockSpec((1, nh, kvd), lambda bi, pt: (bi, 0, 0)),
            pl.BlockSpec((kvh, nh, 1), lambda bi, pt: (0, 0, 0)),
            pl.BlockSpec((1, ih, di), lambda bi, pt: (bi, 0, 0)),
            pl.BlockSpec((1, ih, 1), lambda bi, pt: (bi, 0, 0)),
            row(kvd), row(kvd), row(di),
            any_spec, any_spec, any_spec,
        ],
        out_specs=pl.BlockSpec((1, nh, dh), lambda bi, pt: (bi, 0, 0)),
        scratch_shapes=[pltpu.VMEM((2, kvd, past), F32), pltpu.VMEM((2, kvd, past), F32),
                        pltpu.VMEM((2, di, past), F32), pltpu.VMEM((SUBLANES, past // SUBLANES), I32),
                        pltpu.SemaphoreType.DMA((2, 3))],
    )
    out = pl.pallas_call(
        functools.partial(_dsa_sample_kernel, layer=layer, n_pages=n_pages, page=page, topk=topk,
                          total_bits=max(1, past.bit_length()), n_groups=kvh, head_dim=dh,
                          key_chunk=key_chunk),
        grid_spec=grid_spec,
        out_shape=jax.ShapeDtypeStruct((db, nh, dh), BF16),
        compiler_params=_params("arbitrary"),
        name="dsa_sample",
    )(page_table, qpad, hsel, qi.reshape(db, ih, di), wi.reshape(db, ih, 1),
      k_new.reshape(db, 1, kvd), v_new.reshape(db, 1, kvd), ki_new.reshape(db, 1, di),
      cache_kt, cache_vt, cache_kit)
    return out.reshape(db, hd)


def _conv_prompt_kernel(x_ref, sh_ref, sc_ref, win_ref, cw_ref, a_ref, st_ref, carry_ref):
    i = pl.program_id(1)
    d = x_ref.shape[-1]
    tm = x_ref.shape[1]
    hb = _modulate(x_ref[0], sh_ref[0], sc_ref[0]).astype(BF16)
    b_gate = _dot(hb, win_ref[:, 0:d])
    z = _dot(hb, win_ref[:, d:2 * d]) * _dot(hb, win_ref[:, 2 * d:3 * d])

    @pl.when(i == 0)
    def _():
        carry_ref[...] = jnp.zeros_like(carry_ref)

    row = lax.broadcasted_iota(I32, (tm, d), 0)
    prev1 = carry_ref[SUBLANES - 1:SUBLANES, :]
    prev2 = carry_ref[SUBLANES - 2:SUBLANES - 1, :]
    z1 = jnp.where(row == 0, prev1, pltpu.roll(z, 1, axis=0))
    z2 = jnp.where(row == 0, prev2, jnp.where(row == 1, prev1, pltpu.roll(z, 2, axis=0)))
    y = cw_ref[0:1, :] * z2 + cw_ref[1:2, :] * z1 + cw_ref[2:3, :] * z
    a_ref[0] = (b_gate * y).astype(BF16)
    tail = z[tm - SUBLANES:tm, :]
    carry_ref[...] = tail
    st_ref[0] = tail


def conv_prompt(x3, shift, scale, w_in, conv_w8, tm=ROW_TILE):
    b, t, d = x3.shape
    mod_spec = pl.BlockSpec((1, 1, d), lambda bi, i: (bi, 0, 0))
    return pl.pallas_call(
        _conv_prompt_kernel,
        grid=(b, t // tm),
        in_specs=[pl.BlockSpec((1, tm, d), lambda bi, i: (bi, i, 0)), mod_spec, mod_spec,
                  pl.BlockSpec(w_in.shape, lambda bi, i: (0, 0), pipeline_mode=pl.Buffered(1)),
                  pl.BlockSpec(conv_w8.shape, lambda bi, i: (0, 0))],
        out_specs=[pl.BlockSpec((1, tm, d), lambda bi, i: (bi, i, 0)),
                   pl.BlockSpec((1, SUBLANES, d), lambda bi, i: (bi, 0, 0))],
        out_shape=[jax.ShapeDtypeStruct((b, t, d), BF16),
                   jax.ShapeDtypeStruct((b, SUBLANES, d), F32)],
        scratch_shapes=[pltpu.VMEM((SUBLANES, d), F32)],
        compiler_params=_params("arbitrary", "arbitrary"),
        name="conv_prompt",
    )(x3, shift, scale, w_in, conv_w8)


def _conv_sample_kernel(x_ref, sh_ref, sc_ref, win_ref, cw_ref, p0_ref, p1_ref, a_ref, z_ref):
    d = x_ref.shape[-1]
    hb = _modulate(x_ref[...], sh_ref[0], sc_ref[0]).astype(BF16)
    b_gate = _dot(hb, win_ref[:, 0:d])
    z = _dot(hb, win_ref[:, d:2 * d]) * _dot(hb, win_ref[:, 2 * d:3 * d])
    y = cw_ref[0:1, :] * p0_ref[...] + cw_ref[1:2, :] * p1_ref[...] + cw_ref[2:3, :] * z
    a_ref[...] = (b_gate * y).astype(BF16)
    z_ref[...] = z


def conv_sample(x, shift, scale, w_in, conv_w8, prefix0, prefix1):
    n, d = x.shape
    full = pl.BlockSpec((n, d), lambda i: (0, 0))
    mod_spec = pl.BlockSpec((1, n, d), lambda i: (0, 0, 0))
    return pl.pallas_call(
        _conv_sample_kernel,
        grid=(1,),
        in_specs=[full, mod_spec, mod_spec,
                  pl.BlockSpec(w_in.shape, lambda i: (0, 0), pipeline_mode=pl.Buffered(1)),
                  pl.BlockSpec(conv_w8.shape, lambda i: (0, 0)), full, full],
        out_specs=[full, full],
        out_shape=[jax.ShapeDtypeStruct((n, d), BF16), jax.ShapeDtypeStruct((n, d), F32)],
        compiler_params=_params("arbitrary"),
        name="conv_sample",
    )(x, shift, scale, w_in, conv_w8, prefix0, prefix1)


def kernel(x_prompt, x_sample, cache_k, cache_v, cache_kidx, state_conv, page_table, c_prompt, c_sample, w_ada, b_ada, w_ffn_in, w_ffn_out, w_attn_in, w_attn_out, q_norm_gain, k_norm_gain, w_conv_in, conv_w, w_conv_out):
    b, t, d = x_prompt.shape
    db, dt, _ = x_sample.shape
    assert dt == 1, "the sample path handles one new token per sequence"
    depth = w_ada.shape[0]
    head_dim = q_norm_gain.shape[-1]
    n_heads = w_attn_out.shape[1] // head_dim
    kvh = cache_k.shape[3]
    idx_dim = cache_kidx.shape[-1]
    hd, kvd = n_heads * head_dim, kvh * head_dim
    idx_heads = (w_attn_in.shape[-1] - hd - 2 * kvd - idx_dim) // (idx_dim + 1)
    ihd = idx_heads * idx_dim
    conv_width = conv_w.shape[1]
    assert conv_width == 3
    dims = dict(n_heads=n_heads, group=n_heads // kvh, head_dim=head_dim, hd=hd, kvd=kvd, ihd=ihd,
                idx_dim=idx_dim, idx_heads=idx_heads, v_rows=head_dim + BF16_SUBLANES,
                qk_scale=head_dim ** -0.5,
                wi_scale=idx_heads ** -0.5 * idx_dim ** -0.5)

    n_c = b + db
    n_c_pad = -(-n_c // SUBLANES) * SUBLANES
    c_all = jnp.concatenate([c_prompt, c_sample, jnp.zeros((n_c_pad - n_c, d), F32)], axis=0)
    mods = ada_mods(c_all, w_ada, b_ada).reshape(depth, n_c_pad, N_MOD, d)

    def prompt_mod(layer, j):
        return mods[layer, :b, j][:, None, :]

    def sample_mod(layer, j):
        return mods[layer, b:n_c, j][None]

    gw = 256 if kvd % 256 == 0 and hd % 256 == 0 else kvd
    gi = jnp.arange(gw) // head_dim
    g_mat = jnp.where(gi[:, None] == gi[None, :], 1.0 / head_dim, 0.0).astype(BF16)

    n_attn, n_pool, page = cache_k.shape[:3]
    cache_kt = jnp.transpose(cache_k, (0, 1, 3, 4, 2)).reshape(n_attn, n_pool, kvd, page)
    cache_vt = jnp.transpose(cache_v, (0, 1, 3, 4, 2)).reshape(n_attn, n_pool, kvd, page)
    cache_kit = jnp.transpose(cache_kidx, (0, 1, 3, 2))

    xp = x_prompt.reshape(b * t, d)
    xs = x_sample.reshape(db, d)
    pk, pv, pki, pconv, sk, sv, ski, sconv = [], [], [], [], [], [], [], []
    n_mixers = 2
    for layer in range(depth):
        li = layer // n_mixers
        w1_in, w1_out = w_ffn_in[layer, 0].astype(BF16), w_ffn_out[layer, 0].astype(BF16)
        w2_in, w2_out = w_ffn_in[layer, 1].astype(BF16), w_ffn_out[layer, 1].astype(BF16)
        pm = [prompt_mod(layer, j) for j in range(N_MOD)]
        sm = [sample_mod(layer, j) for j in range(N_MOD)]
        xp = ffn_call(xp, pm[0], pm[1], pm[2], w1_in, w1_out)
        xs = ffn_call(xs, sm[0], sm[1], sm[2], w1_in, w1_out)
        if layer % n_mixers == 0:
            wa = w_attn_in[li].astype(BF16)
            o1, o2, o3, o4, o5 = hd, hd + kvd, hd + 2 * kvd, hd + 2 * kvd + ihd, hd + 2 * kvd + ihd + idx_dim
            wkw = jnp.pad(wa[:, o4:], ((0, 0), (0, 128 - (idx_dim + idx_heads))))
            aw = dict(wq=wa[:, :o1], wk=wa[:, o1:o2], wv=wa[:, o2:o3], wqi=wa[:, o3:o4], wkw=wkw,
                      gq=jnp.tile(q_norm_gain[li], n_heads)[None, :],
                      gk=jnp.tile(k_norm_gain[li], kvh)[None, :], g=g_mat)
            w_mix_out = w_attn_out[li].astype(BF16)
            qt, k, kb, v, vt, qit, ki, kib, wit = attn_proj_prompt(xp.reshape(b, t, d), pm[3], pm[4], aw, dims)
            ap = dsa_prompt(qt, qit, wit, kb, vt, kib, dims).reshape(b * t, hd)
            pk.append(k.reshape(b, t, kvh, head_dim))
            pv.append(v.reshape(b, t, kvh, head_dim))
            pki.append(ki)
            qs, ks, vs, qis, kis, wis = attn_proj_sample(xs, sm[3], sm[4], aw, dims)
            a_s = dsa_sample(qs, ks, vs, qis, kis, wis, cache_kt, cache_vt, cache_kit, li,
                             page_table, dims)
            sk.append(ks.reshape(db, 1, kvh, head_dim))
            sv.append(vs.reshape(db, 1, kvh, head_dim))
            ski.append(kis.reshape(db, 1, idx_dim))
        else:
            wc_in = w_conv_in[li].astype(BF16)
            w_mix_out = w_conv_out[li].astype(BF16)
            cw8 = jnp.pad(conv_w[li], ((0, SUBLANES - conv_width), (0, 0)))
            ap, st = conv_prompt(xp.reshape(b, t, d), pm[3], pm[4], wc_in, cw8)
            ap = ap.reshape(b * t, d)
            pconv.append(st[:, SUBLANES - (conv_width - 1):, :])
            prefix = state_conv[li]
            a_s, zs = conv_sample(xs, sm[3], sm[4], wc_in, cw8, prefix[:, 0], prefix[:, 1])
            sconv.append(jnp.stack([prefix[:, 1], zs], axis=1))
        xp = ffn_call(xp, pm[6], pm[7], pm[8], w2_in, w2_out, pre=(ap, w_mix_out, pm[5]))
        xs = ffn_call(xs, sm[6], sm[7], sm[8], w2_in, w2_out, pre=(a_s, w_mix_out, sm[5]))

    return (xp.reshape(b, t, d), xs.reshape(db, 1, d),
            jnp.stack(pk), jnp.stack(pv), jnp.stack(pki), jnp.stack(pconv),
            jnp.stack(sk), jnp.stack(sv), jnp.stack(ski), jnp.stack(sconv))
```

```python
import functools

import jax
import jax.numpy as jnp
from jax import lax
from jax.experimental import pallas as pl
from jax.experimental.pallas import tpu as pltpu

F32 = jnp.float32
BF16 = jnp.bfloat16
I32 = jnp.int32
I16 = jnp.int16

RMS_EPS = 1e-6
TOPK_MAX = 256
N_MOD = 9
SUBLANES = 8
BF16_SUBLANES = 16
LOG2_E = 1.4426950408889634
NEG_INF = float("-inf")
INT_MIN = -(2 ** 31)
INT16_MIN = -(2 ** 15)
HALF16 = 2 ** 15
KEY_NEG_INF = (0xFF800000 ^ 0x7FFFFFFF) - (1 << 32)

VMEM_LIMIT_BYTES = 56 * 1024 * 1024
ROW_TILE = 256
FFN_ROW_TILE = 1024
FF_CHUNK = 256
SAMPLE_KEY_CHUNK = 2048
SELECT_CHUNKS = 2
RADIX_BITS = 4


def _params(*sem):
    return pltpu.CompilerParams(dimension_semantics=sem, vmem_limit_bytes=VMEM_LIMIT_BYTES)


def _sigmoid(x):
    return 1.0 / (1.0 + jnp.exp(-x))


def _modulate(x, shift, scale):
    ms = jnp.mean(x * x, axis=-1, keepdims=True)
    return (x * lax.rsqrt(ms + RMS_EPS)) * (1.0 + scale) + shift


def _dot(a, b):
    return jnp.dot(a, b, preferred_element_type=F32)


def _dot_nt(a, b):
    return lax.dot_general(a, b, (((1,), (1,)), ((), ())), preferred_element_type=F32)


def _float_key(x):
    bits = lax.bitcast_convert_type(x, I32)
    return bits ^ (lax.shift_right_arithmetic(bits, 31) & 0x7FFFFFFF)


def _fold_rows(x, op, rows=SUBLANES):
    parts = [x[j * rows:(j + 1) * rows] for j in range(x.shape[0] // rows)]
    while len(parts) > 1:
        parts = [op(parts[j], parts[j + 1]) if j + 1 < len(parts) else parts[j]
                 for j in range(0, len(parts), 2)]
    return parts[0]


def _ada_kernel(c_ref, w_ref, b_ref, o_ref):
    c = c_ref[...]
    s = (c * _sigmoid(c)).astype(BF16)
    o_ref[0] = _dot(s, w_ref[0].astype(BF16)) + b_ref[0]


def ada_mods(c_all, w_ada, b_ada):
    depth, d, nd = w_ada.shape
    r = c_all.shape[0]
    return pl.pallas_call(
        _ada_kernel,
        grid=(depth, nd // d),
        in_specs=[
            pl.BlockSpec((r, d), lambda l, j: (0, 0)),
            pl.BlockSpec((1, d, d), lambda l, j: (l, 0, j)),
            pl.BlockSpec((1, 1, d), lambda l, j: (l, 0, j)),
        ],
        out_specs=pl.BlockSpec((1, r, d), lambda l, j: (l, 0, j)),
        out_shape=jax.ShapeDtypeStruct((depth, r, nd), F32),
        compiler_params=_params("arbitrary", "arbitrary"),
        name="ada_mods",
    )(c_all, w_ada, b_ada.reshape(depth, 1, nd))


def _ffn_kernel(*refs, has_pre, d_ff, tf):
    if has_pre:
        (x_ref, a_ref, wpre_ref, gpre_ref, sh_ref, sc_ref, g_ref, win_ref, wout_ref,
         o_ref, acc_ref) = refs
    else:
        x_ref, sh_ref, sc_ref, g_ref, win_ref, wout_ref, o_ref, acc_ref = refs
    x = x_ref[...]
    if has_pre:
        x = x + gpre_ref[0] * _dot(a_ref[...], wpre_ref[...])
    hb = _modulate(x, sh_ref[0], sc_ref[0]).astype(BF16)
    for c in range(d_ff // tf):
        u = _dot(hb, win_ref[:, c * tf:(c + 1) * tf])
        g = _dot(hb, win_ref[:, d_ff + c * tf:d_ff + (c + 1) * tf])
        act = ((u * _sigmoid(u)) * g).astype(BF16)
        y = _dot(act, wout_ref[c * tf:(c + 1) * tf, :])
        if c == 0:
            acc_ref[...] = y
        else:
            acc_ref[...] += y
    o_ref[...] = x + 0.5 * g_ref[0] * acc_ref[...]


def ffn_call(x, shift, scale, gate, w_in, w_out, which, pre=None, tm=FFN_ROW_TILE):
    n, d = x.shape
    tm = min(tm, n)
    groups = shift.shape[0]
    tiles_per_group = n // tm // groups
    d_ff = w_out.shape[2]
    tf = FF_CHUNK if d_ff % FF_CHUNK == 0 else d_ff

    def picked(w):
        return pl.BlockSpec((None, None) + w.shape[2:], lambda i: which + (0, 0),
                            pipeline_mode=pl.Buffered(1))

    def row_spec(width):
        return pl.BlockSpec((tm, width), lambda i: (i, 0))

    def mod_spec(m):
        return pl.BlockSpec((1,) + m.shape[1:], lambda i: (i // tiles_per_group, 0, 0))

    def whole(w):
        return pl.BlockSpec(w.shape, lambda i: (0, 0), pipeline_mode=pl.Buffered(1))

    args, specs = [x], [row_spec(d)]
    if pre is not None:
        a, w_pre, g_pre = pre
        args += [a, w_pre, g_pre]
        specs += [row_spec(a.shape[1]), whole(w_pre), mod_spec(g_pre)]
    args += [shift, scale, gate, w_in, w_out]
    specs += [mod_spec(shift), mod_spec(scale), mod_spec(gate), picked(w_in), picked(w_out)]
    return pl.pallas_call(
        functools.partial(_ffn_kernel, has_pre=pre is not None, d_ff=d_ff, tf=tf),
        grid=(n // tm,),
        in_specs=specs,
        out_specs=row_spec(d),
        out_shape=jax.ShapeDtypeStruct((n, d), F32),
        scratch_shapes=[pltpu.VMEM((tm, d), F32)],
        compiler_params=_params("arbitrary"),
        name="ffn_pre" if pre is not None else "ffn",
    )(*args)


def _head_mean_square(y, g_ref):
    y2 = (y * y).astype(BF16)
    w = g_ref.shape[0]
    parts = [_dot(y2[:, j * w:(j + 1) * w], g_ref[...]) for j in range(y.shape[1] // w)]
    return parts[0] if len(parts) == 1 else jnp.concatenate(parts, axis=1)


def _project(x, sh, sc, wq_ref, wk_ref, wv_ref, wqi_ref, wkw_ref, gq_ref, gk_ref, g_ref,
             qk_scale):
    hb = _modulate(x, sh, sc).astype(BF16)
    q = _dot(hb, wq_ref[...])
    qn = q * lax.rsqrt(_head_mean_square(q, g_ref) + RMS_EPS) * gq_ref[...] * qk_scale
    k = _dot(hb, wk_ref[...])
    kn = k * lax.rsqrt(_head_mean_square(k, g_ref) + RMS_EPS) * gk_ref[...]
    v = _dot(hb, wv_ref[...])
    qi = _dot(hb, wqi_ref[...])
    kw = _dot(hb, wkw_ref[...])
    return qn, kn, v, qi, kw


def _proj_kernel(x_ref, sh_ref, sc_ref, wq_ref, wk_ref, wv_ref, wqi_ref, wkw_ref, gq_ref,
                 gk_ref, g_ref,
                 qt_ref, k_ref, kb_ref, v_ref, vt_ref, qit_ref, ki_ref, kib_ref, wit_ref,
                 *, idx_dim, idx_heads, head_dim, v_rows, qk_scale, wi_scale):
    qn, kn, v, qi, kw = _project(x_ref[0], sh_ref[0], sc_ref[0], wq_ref, wk_ref, wv_ref,
                                 wqi_ref, wkw_ref, gq_ref, gk_ref, g_ref, qk_scale)
    tm = x_ref.shape[1]
    qt_ref[0, 0] = qn.T.astype(BF16)
    k_ref[0] = kn
    kb_ref[0] = kn.astype(BF16)
    v_ref[0] = v
    vt = v.T.astype(BF16)
    ones_rows = (lax.broadcasted_iota(I32, (v_rows - head_dim, tm), 0) == 0).astype(BF16)
    for g in range(v.shape[1] // head_dim):
        vt_ref[0, 0, g * v_rows:g * v_rows + head_dim, :] = vt[g * head_dim:(g + 1) * head_dim]
        vt_ref[0, 0, g * v_rows + head_dim:(g + 1) * v_rows, :] = ones_rows
    qit_ref[0, 0] = qi.T.astype(BF16)
    ki = kw[:, :idx_dim]
    ki_ref[0] = ki
    kib_ref[0] = ki.astype(BF16)
    wit_ref[0, 0] = kw.T[idx_dim:idx_dim + idx_heads, :] * wi_scale


def _sample_proj_kernel(x_ref, sh_ref, sc_ref, wq_ref, wk_ref, wv_ref, wqi_ref, wkw_ref,
                        gq_ref, gk_ref, g_ref,
                        q_ref, k_ref, v_ref, qi_ref, ki_ref, wi_ref,
                        *, idx_dim, idx_heads, qk_scale, wi_scale):
    qn, kn, v, qi, kw = _project(x_ref[...], sh_ref[0], sc_ref[0], wq_ref, wk_ref, wv_ref,
                                 wqi_ref, wkw_ref, gq_ref, gk_ref, g_ref, qk_scale)
    q_ref[...] = qn.astype(BF16)
    k_ref[...] = kn
    v_ref[...] = v
    qi_ref[...] = qi.astype(BF16)
    ki_ref[...] = kw[:, :idx_dim]
    wi_ref[...] = kw[:, idx_dim:idx_dim + idx_heads] * wi_scale


def _proj_weight_specs(ws, n_grid):
    zeros = (lambda *idx: (0, 0))
    return [pl.BlockSpec(w.shape, zeros, pipeline_mode=pl.Buffered(1)) for w in ws]


def attn_proj_prompt(x3, shift, scale, aw, dims, tm=ROW_TILE):
    b, t, d = x3.shape
    nt = t // tm
    hd, kvd, ihd, di, ih = dims["hd"], dims["kvd"], dims["ihd"], dims["idx_dim"], dims["idx_heads"]
    ws = [aw["wq"], aw["wk"], aw["wv"], aw["wqi"], aw["wkw"], aw["gq"], aw["gk"], aw["g"]]
    mod_spec = pl.BlockSpec((1, 1, d), lambda bi, i: (bi, 0, 0))
    nat = lambda w: pl.BlockSpec((1, tm, w), lambda bi, i: (bi, i, 0))
    tr = lambda r: pl.BlockSpec((1, 1, r, tm), lambda bi, i: (bi, i, 0, 0))
    vtd = kvd // dims["head_dim"] * dims["v_rows"]
    return pl.pallas_call(
        functools.partial(_proj_kernel, idx_dim=di, idx_heads=ih, head_dim=dims["head_dim"],
                          v_rows=dims["v_rows"], qk_scale=dims["qk_scale"] * LOG2_E,
                          wi_scale=dims["wi_scale"]),
        grid=(b, nt),
        in_specs=[nat(d), mod_spec, mod_spec] + _proj_weight_specs(ws, 2),
        out_specs=[tr(hd), nat(kvd), nat(kvd), nat(kvd), tr(vtd), tr(ihd), nat(di), nat(di), tr(ih)],
        out_shape=[
            jax.ShapeDtypeStruct((b, nt, hd, tm), BF16),
            jax.ShapeDtypeStruct((b, t, kvd), F32),
            jax.ShapeDtypeStruct((b, t, kvd), BF16),
            jax.ShapeDtypeStruct((b, t, kvd), F32),
            jax.ShapeDtypeStruct((b, nt, vtd, tm), BF16),
            jax.ShapeDtypeStruct((b, nt, ihd, tm), BF16),
            jax.ShapeDtypeStruct((b, t, di), F32),
            jax.ShapeDtypeStruct((b, t, di), BF16),
            jax.ShapeDtypeStruct((b, nt, ih, tm), F32),
        ],
        compiler_params=_params("arbitrary", "arbitrary"),
        name="attn_proj",
    )(x3, shift, scale, *ws)


def attn_proj_sample(x, shift, scale, aw, dims):
    n, d = x.shape
    hd, kvd, ihd, di, ih = dims["hd"], dims["kvd"], dims["ihd"], dims["idx_dim"], dims["idx_heads"]
    ws = [aw["wq"], aw["wk"], aw["wv"], aw["wqi"], aw["wkw"], aw["gq"], aw["gk"], aw["g"]]
    full = lambda w: pl.BlockSpec((n, w), lambda i: (0, 0))
    mod_spec = pl.BlockSpec((1, n, d), lambda i: (0, 0, 0))
    return pl.pallas_call(
        functools.partial(_sample_proj_kernel, idx_dim=di, idx_heads=ih,
                          qk_scale=dims["qk_scale"], wi_scale=dims["wi_scale"]),
        grid=(1,),
        in_specs=[full(d), mod_spec, mod_spec] + _proj_weight_specs(ws, 1),
        out_specs=[full(hd), full(kvd), full(kvd), full(ihd), full(di), full(ih)],
        out_shape=[
            jax.ShapeDtypeStruct((n, hd), BF16),
            jax.ShapeDtypeStruct((n, kvd), F32),
            jax.ShapeDtypeStruct((n, kvd), F32),
            jax.ShapeDtypeStruct((n, ihd), BF16),
            jax.ShapeDtypeStruct((n, di), F32),
            jax.ShapeDtypeStruct((n, ih), F32),
        ],
        compiler_params=_params("arbitrary"),
        name="attn_proj_sample",
    )(x, shift, scale, *ws)


def _dsa_kernel(qt_ref, qit_ref, wit_ref, k_ref, vt_ref, ki_ref, o_ref,
                keys_ref, hi_ref, lo_ref, qpad_ref, s_ref, acc_ref, m_ref, mnew_ref, l_ref, jlim_ref,
                *, n_heads, group, head_dim, v_rows, idx_heads, idx_dim, topk, seq_bits):
    tq = qt_ref.shape[-1]
    tk = tq
    tk2 = SELECT_CHUNKS * tk
    i = pl.program_id(1)
    n_chunks = i + 1
    n_pairs = (n_chunks + SELECT_CHUNKS - 1) // SELECT_CHUNKS
    q_idx = i * tq + lax.broadcasted_iota(I32, (tk, tq), 1)
    row_iota = lax.broadcasted_iota(I32, (tk, tq), 0)
    q_idx2 = i * tq + lax.broadcasted_iota(I32, (tk2, tq), 1)
    row_iota2 = lax.broadcasted_iota(I32, (tk2, tq), 0)

    def chunk_rows(c):
        return pl.ds(pl.multiple_of(c * tk, tk), tk)

    def pair_rows(j):
        return pl.ds(pl.multiple_of(j * tk2, tk2), tk2)

    def score_chunk(c, carry):
        kic = ki_ref[0, chunk_rows(c), :]
        acc = jnp.zeros((tk, tq), F32)
        for h in range(idx_heads):
            s = _dot(kic, qit_ref[0, 0, h * idx_dim:(h + 1) * idx_dim, :])
            acc = acc + jnp.maximum(s, 0.0) * wit_ref[0, 0, h:h + 1, :]
        kidx = c * tk + row_iota
        key = _float_key(jnp.where(kidx <= q_idx, acc, NEG_INF))
        span = 1 << seq_bits
        key = jnp.where(key > 0, key + span, jnp.where(key == 0, (span - 1) - kidx, key))
        keys_ref[chunk_rows(c), :] = key
        hi_ref[chunk_rows(c), :] = lax.shift_right_arithmetic(key, 16).astype(I16)
        lo_ref[chunk_rows(c), :] = ((key & 0xFFFF) - HALF16).astype(I16)
        return carry

    lax.fori_loop(0, n_chunks, score_chunk, 0)

    def pad_chunk(c, carry):
        keys_ref[chunk_rows(c), :] = jnp.full((tk, tq), INT_MIN, I32)
        hi_ref[chunk_rows(c), :] = jnp.full((tk, tq), INT16_MIN, I16)
        lo_ref[chunk_rows(c), :] = jnp.full((tk, tq), INT16_MIN, I16)
        return carry

    lax.fori_loop(n_chunks, n_pairs * SELECT_CHUNKS, pad_chunk, 0)

    def count(pred):
        def body(j, cnt):
            kidx = j * tk2 + row_iota2
            return cnt + _fold_rows(pred(keys_ref[pair_rows(j), :], kidx).astype(I32), jnp.add)
        cnt8 = lax.fori_loop(0, n_pairs, body, jnp.zeros((SUBLANES, tq), I32))
        return jnp.sum(cnt8, axis=0, keepdims=True)

    def count_ge16(ref, cand):
        cand16 = cand.astype(I16)
        def body(j, cnt):
            m = (ref[pair_rows(j), :] >= cand16).astype(I16)
            return cnt + _fold_rows(m, jnp.add, BF16_SUBLANES).astype(I32)
        cnt16 = lax.fori_loop(0, n_pairs, body, jnp.zeros((BF16_SUBLANES, tq), I32))
        return jnp.sum(cnt16, axis=0, keepdims=True)

    def bisect16(ref, k_need, n_all):
        zero = jnp.zeros((1, tq), I32)
        c0 = count_ge16(ref, zero)
        nonneg = c0 >= k_need
        init = (jnp.where(nonneg, zero, INT16_MIN), jnp.where(nonneg, c0, n_all),
                jnp.where(nonneg, zero, c0))

        def search(it, carry):
            prefix, n_ge, n_up = carry
            cand = prefix | jnp.left_shift(jnp.int32(1), 14 - it)
            c = count_ge16(ref, cand)
            ok = c >= k_need
            return jnp.where(ok, cand, prefix), jnp.where(ok, c, n_ge), jnp.where(ok, n_up, c)

        return lax.fori_loop(0, 15, search, init)

    zero = jnp.zeros((1, tq), I32)
    thr_hi, n_ge_hi, n_gt_hi = bisect16(hi_ref, topk, n_chunks * tk)

    thr_hi16 = thr_hi.astype(I16)

    def mask_low(j, carry):
        keep = hi_ref[pair_rows(j), :] == thr_hi16
        lo_ref[pair_rows(j), :] = jnp.where(keep, lo_ref[pair_rows(j), :], INT16_MIN).astype(I16)
        return carry

    lax.fori_loop(0, n_pairs, mask_low, 0)
    thr_lo, n_ge_lo, n_gt_lo = bisect16(lo_ref, topk - n_gt_hi, n_ge_hi - n_gt_hi)
    thr = lax.shift_left(thr_hi, 16) | (thr_lo + HALF16)
    n_ge = n_gt_hi + n_ge_lo
    n_gt = n_gt_hi + n_gt_lo
    need = topk - n_gt

    jlim_ref[...] = jnp.full((1, tq), (1 << seq_bits) - 1, I32)
    excess = jnp.logical_and(n_ge > topk, thr > KEY_NEG_INF)

    @pl.when(jnp.max(excess.astype(I32)) > 0)
    def _():
        def tie_search(it, j):
            cand = j | jnp.left_shift(jnp.int32(1), seq_bits - 1 - it)
            n_before = count(lambda kc, kidx: jnp.logical_and(kc == thr, kidx < cand))
            return jnp.where(n_before < need, cand, j)
        jlim_ref[...] = lax.fori_loop(0, seq_bits, tie_search, zero)

    jlim = jlim_ref[...]

    def bias_pair(j, carry):
        kc = keys_ref[pair_rows(j), :]
        kidx = j * tk2 + row_iota2
        sel = jnp.logical_or(kc > thr, jnp.logical_and(kc == thr, kidx <= jlim))
        sel = jnp.logical_and(sel, kidx <= q_idx2)
        bias = jnp.where(sel, 0.0, NEG_INF).astype(F32)
        keys_ref[pair_rows(j), :] = lax.bitcast_convert_type(bias, I32)
        return carry

    lax.fori_loop(0, n_pairs, bias_pair, 0)

    qpad_ref[...] = jnp.zeros(qpad_ref.shape, BF16)
    for h in range(n_heads):
        g = h // group
        qpad_ref[h, g * head_dim:(g + 1) * head_dim, :] = qt_ref[0, 0, h * head_dim:(h + 1) * head_dim, :]
    m_ref[...] = jnp.full(m_ref.shape, NEG_INF, F32)
    l_ref[...] = jnp.zeros(l_ref.shape, F32)
    acc_ref[...] = jnp.zeros(acc_ref.shape, F32)

    def logits_head(h, kc, bias, buf, m_prev):
        s = _dot(kc, qpad_ref[h]) + bias
        s_ref[buf, h] = s
        mnew_ref[buf, h:h + 1, :] = jnp.maximum(
            m_prev, jnp.max(_fold_rows(s, jnp.maximum), axis=0, keepdims=True))

    def weights_head(h, c, buf):
        g = h // group
        rows = slice(h * head_dim, (h + 1) * head_dim)
        m_old = m_ref[h:h + 1, :]
        m_new = mnew_ref[buf, h:h + 1, :]
        m_safe = jnp.where(m_new == NEG_INF, 0.0, m_new)
        p = jnp.exp2(s_ref[buf, h] - m_safe).astype(BF16)
        alpha = jnp.exp2(m_old - m_safe)
        pv = _dot(vt_ref[0, c, g * v_rows:(g + 1) * v_rows, :], p)
        acc_ref[rows, :] = alpha * acc_ref[rows, :] + pv[0:head_dim]
        l_ref[h:h + 1, :] = alpha * l_ref[h:h + 1, :] + pv[head_dim:head_dim + 1]
        m_ref[h:h + 1, :] = m_new

    def chunk_operands(c):
        return k_ref[0, chunk_rows(c), :], lax.bitcast_convert_type(keys_ref[chunk_rows(c), :], F32)

    kc0, bias0 = chunk_operands(0)
    for h in range(n_heads):
        logits_head(h, kc0, bias0, 0, m_ref[h:h + 1, :])

    def step(c, src, dst):
        kc, bias = chunk_operands(c + 1)
        for h in range(n_heads):
            logits_head(h, kc, bias, dst, mnew_ref[src, h:h + 1, :])
            weights_head(h, c, src)

    def last_weights(buf):
        for h in range(n_heads):
            weights_head(h, n_chunks - 1, buf)

    def two_steps(j, carry):
        step(2 * j, 0, 1)
        step(2 * j + 1, 1, 0)
        return carry

    n_steps = n_chunks - 1
    lax.fori_loop(0, n_steps // 2, two_steps, 0)

    @pl.when(n_steps % 2 == 1)
    def _():
        step(n_steps - 1, 0, 1)
        last_weights(1)

    @pl.when(n_steps % 2 == 0)
    def _():
        last_weights(0)

    for h in range(n_heads):
        rows = slice(h * head_dim, (h + 1) * head_dim)
        acc_ref[rows, :] = acc_ref[rows, :] / l_ref[h:h + 1, :]
    o_ref[0] = acc_ref[...].T.astype(BF16)


def dsa_prompt(qt, qit, wit, kb, vt, kib, dims):
    b, nt, hd, tq = qt.shape
    t = nt * tq
    kvd, ihd, ih = kb.shape[-1], qit.shape[2], wit.shape[2]
    nh = dims["n_heads"]
    topk = min(TOPK_MAX, t // 4)
    key_rows = -(-nt // SELECT_CHUNKS) * SELECT_CHUNKS * tq
    blk = lambda r: pl.BlockSpec((1, 1, r, tq), lambda bi, i: (bi, i, 0, 0))
    per_batch3 = lambda w: pl.BlockSpec((1, t, w), lambda bi, i: (bi, 0, 0),
                                        pipeline_mode=pl.Buffered(1))
    return pl.pallas_call(
        functools.partial(_dsa_kernel, n_heads=nh, group=dims["group"],
                          head_dim=dims["head_dim"], v_rows=dims["v_rows"], idx_heads=ih,
                          idx_dim=dims["idx_dim"], topk=topk, seq_bits=max(1, (t - 1).bit_length())),
        grid=(b, nt),
        in_specs=[blk(hd), blk(ihd), blk(ih), per_batch3(kvd),
                  pl.BlockSpec((1, nt) + vt.shape[2:], lambda bi, i: (bi, 0, 0, 0),
                               pipeline_mode=pl.Buffered(1)),
                  per_batch3(dims["idx_dim"])],
        out_specs=pl.BlockSpec((1, tq, hd), lambda bi, i: (bi, i, 0)),
        out_shape=jax.ShapeDtypeStruct((b, t, hd), BF16),
        scratch_shapes=[pltpu.VMEM((key_rows, tq), I32), pltpu.VMEM((key_rows, tq), I16),
                        pltpu.VMEM((key_rows, tq), I16), pltpu.VMEM((nh, kvd, tq), BF16),
                        pltpu.VMEM((2, nh, tq, tq), F32), pltpu.VMEM((hd, tq), F32),
                        pltpu.VMEM((nh, tq), F32), pltpu.VMEM((2, nh, tq), F32),
                        pltpu.VMEM((nh, tq), F32), pltpu.VMEM((1, tq), I32)],
        compiler_params=_params("arbitrary", "arbitrary"),
        name="dsa_prompt",
    )(qt, qit, wit, kb, vt, kib)


def _dsa_sample_kernel(pt_ref, qpad_ref, hsel_ref, qi_ref, wi_ref, knew_ref, vnew_ref, kinew_ref,
                       ckt_ref, cvt_ref, ckit_ref, o_ref,
                       ktbuf, vtbuf, kitbuf, dense_ref, sem,
                       *, layer, n_pages, page, topk, total_bits, n_groups, head_dim, key_chunk):
    bi = pl.program_id(0)
    n_seq = pl.num_programs(0)
    past = n_pages * page
    slot = bi % 2

    def page_copies(seq, buf, p):
        phys = pt_ref[seq, p]
        cols = pl.ds(pl.multiple_of(p * page, page), page)
        return (pltpu.make_async_copy(ckt_ref.at[layer, phys], ktbuf.at[buf, :, cols], sem.at[buf, 0]),
                pltpu.make_async_copy(cvt_ref.at[layer, phys], vtbuf.at[buf, :, cols], sem.at[buf, 1]),
                pltpu.make_async_copy(ckit_ref.at[layer, phys], kitbuf.at[buf, :, cols], sem.at[buf, 2]))

    def start_seq(seq, buf):
        def body(p, carry):
            for cp in page_copies(seq, buf, p):
                cp.start()
            return carry
        lax.fori_loop(0, n_pages, body, 0)

    def wait_seq(seq, buf):
        def body(p, carry):
            for cp in page_copies(seq, buf, p):
                cp.wait()
            return carry
        lax.fori_loop(0, n_pages, body, 0)

    @pl.when(bi == 0)
    def _():
        start_seq(0, 0)

    @pl.when(bi + 1 < n_seq)
    def _():
        start_seq(bi + 1, 1 - slot)

    wait_seq(bi, slot)
    chunks = [slice(c * key_chunk, (c + 1) * key_chunk) for c in range(past // key_chunk)]

    qi = qi_ref[0]
    wi = wi_ref[0]
    score = jnp.concatenate(
        [jnp.sum(jnp.maximum(_dot(qi, kitbuf[slot, :, ch].astype(BF16)), 0.0) * wi, axis=0,
                 keepdims=True) for ch in chunks], axis=1) + 0.0
    s_new = jnp.sum(qi.astype(F32) * kinew_ref[0], axis=1, keepdims=True)
    score_new = jnp.sum(jnp.maximum(s_new, 0.0) * wi, axis=0, keepdims=True) + 0.0
    keys = _float_key(score)
    key_new = _float_key(score_new)
    kidx = lax.broadcasted_iota(I32, (1, past), 1)

    width = past // SUBLANES
    for r in range(SUBLANES):
        dense_ref[r:r + 1, :] = keys[:, r * width:(r + 1) * width]
    keys_d = dense_ref[...]
    kidx_d = (lax.broadcasted_iota(I32, (SUBLANES, width), 0) * width
              + lax.broadcasted_iota(I32, (SUBLANES, width), 1))
    idx_new = jnp.full((1, 1), past, I32)

    def count(pred):
        n = jnp.sum(pred(keys_d, kidx_d).astype(I32), axis=1, keepdims=True)
        return jnp.sum(n, axis=0, keepdims=True) + pred(key_new, idx_new).astype(I32)

    def digits(base, n_bits, holds):
        shift = n_bits
        while shift > 0:
            step = min(RADIX_BITS, shift)
            shift -= step
            digit = jnp.zeros((1, 1), I32)
            for j in range(1, 1 << step):
                digit = digit + holds(base | (j << shift)).astype(I32)
            base = base | lax.shift_left(digit, shift)
        return base

    zero = jnp.zeros((1, 1), I32)
    sign = jnp.where(count(lambda k, j: k >= zero) >= topk, zero, INT_MIN)
    thr = digits(sign, 31, lambda cand: count(lambda k, j: k >= cand) >= topk)
    need = topk - count(lambda k, j: k > thr)
    jlim = digits(zero, total_bits, lambda cand: count(
        lambda k, j: jnp.logical_and(k == thr, j < cand)) < need)

    def selected(k, j):
        return jnp.logical_or(k > thr, jnp.logical_and(k == thr, j <= jlim))

    sel = selected(keys, kidx)
    sel_new = selected(key_new, jnp.full((1, 1), past, I32))

    qpad = qpad_ref[0]
    logits = jnp.concatenate([_dot(qpad, ktbuf[slot, :, ch].astype(BF16)) for ch in chunks], axis=1)
    logits = jnp.where(sel, logits, NEG_INF)
    logit_new = jnp.sum(qpad.astype(F32) * knew_ref[0], axis=1, keepdims=True)
    logit_new = jnp.where(sel_new, logit_new, NEG_INF)
    m = jnp.maximum(jnp.max(logits, axis=1, keepdims=True), logit_new)
    p = jnp.exp(logits - m)
    p_new = jnp.exp(logit_new - m)
    denom = jnp.sum(p, axis=1, keepdims=True) + p_new
    pb = p.astype(BF16)
    o_all = p_new * vnew_ref[0]
    for ch in chunks:
        o_all = o_all + _dot_nt(pb[:, ch], vtbuf[slot, :, ch].astype(BF16))
    o_all = o_all / denom
    out = jnp.zeros((o_all.shape[0], head_dim), F32)
    for g in range(n_groups):
        out = out + hsel_ref[g] * o_all[:, g * head_dim:(g + 1) * head_dim]
    o_ref[0] = out.astype(BF16)


def dsa_sample(q, k_new, v_new, qi, ki_new, wi, cache_kt, cache_vt, cache_kit, layer, page_table, dims):
    db, hd = q.shape
    n_layers, n_pool, kvd, page = cache_kt.shape
    dh = dims["head_dim"]
    kvh = kvd // dh
    n_pages = page_table.shape[1]
    past = n_pages * page
    nh, group, ih, di = dims["n_heads"], dims["group"], dims["idx_heads"], dims["idx_dim"]
    topk = min(TOPK_MAX, (past + 1) // 4)
    key_chunk = SAMPLE_KEY_CHUNK if past % SAMPLE_KEY_CHUNK == 0 else past
    head_group = jnp.arange(nh) // group
    onehot = (head_group[:, None] == jnp.arange(kvh)[None, :])
    qpad = (q.reshape(db, nh, 1, dh) * onehot[None, :, :, None].astype(q.dtype)).reshape(db, nh, kvd)
    hsel = jnp.transpose(onehot.astype(F32))[:, :, None]
    row = lambda w: pl.BlockSpec((1, 1, w), lambda bi, pt: (bi, 0, 0))
    any_spec = pl.BlockSpec(memory_space=pl.ANY)
    grid_spec = pltpu.PrefetchScalarGridSpec(
        num_scalar_prefetch=1,
        grid=(db,),
        in_specs=[
            pl.BlockSpec((1, nh, kvd), lambda bi, pt: (bi, 0, 0)),
            pl.BlockSpec((kvh, nh, 1), lambda bi, pt: (0, 0, 0)),
            pl.BlockSpec((1, ih, di), lambda bi, pt: (bi, 0, 0)),
            pl.BlockSpec((1, ih, 1), lambda bi, pt: (bi, 0, 0)),
            row(kvd), row(kvd), row(di),
            any_spec, any_spec, any_spec,
        ],
        out_specs=pl.BlockSpec((1, nh, dh), lambda bi, pt: (bi, 0, 0)),
        scratch_shapes=[pltpu.VMEM((2, kvd, past), F32), pltpu.VMEM((2, kvd, past), F32),
                        pltpu.VMEM((2, di, past), F32), pltpu.VMEM((SUBLANES, past // SUBLANES), I32),
                        pltpu.SemaphoreType.DMA((2, 3))],
    )
    out = pl.pallas_call(
        functools.partial(_dsa_sample_kernel, layer=layer, n_pages=n_pages, page=page, topk=topk,
                          total_bits=max(1, past.bit_length()), n_groups=kvh, head_dim=dh,
                          key_chunk=key_chunk),
        grid_spec=grid_spec,
        out_shape=jax.ShapeDtypeStruct((db, nh, dh), BF16),
        compiler_params=_params("arbitrary"),
        name="dsa_sample",
    )(page_table, qpad, hsel, qi.reshape(db, ih, di), wi.reshape(db, ih, 1),
      k_new.reshape(db, 1, kvd), v_new.reshape(db, 1, kvd), ki_new.reshape(db, 1, di),
      cache_kt, cache_vt, cache_kit)
    return out.reshape(db, hd)


def _conv_prompt_kernel(x_ref, sh_ref, sc_ref, win_ref, cw_ref, a_ref, st_ref, carry_ref):
    i = pl.program_id(1)
    d = x_ref.shape[-1]
    tm = x_ref.shape[1]
    hb = _modulate(x_ref[0], sh_ref[0], sc_ref[0]).astype(BF16)
    b_gate = _dot(hb, win_ref[:, 0:d])
    z = _dot(hb, win_ref[:, d:2 * d]) * _dot(hb, win_ref[:, 2 * d:3 * d])

    @pl.when(i == 0)
    def _():
        carry_ref[...] = jnp.zeros_like(carry_ref)

    row = lax.broadcasted_iota(I32, (tm, d), 0)
    prev1 = carry_ref[SUBLANES - 1:SUBLANES, :]
    prev2 = carry_ref[SUBLANES - 2:SUBLANES - 1, :]
    z1 = jnp.where(row == 0, prev1, pltpu.roll(z, 1, axis=0))
    z2 = jnp.where(row == 0, prev2, jnp.where(row == 1, prev1, pltpu.roll(z, 2, axis=0)))
    y = cw_ref[0:1, :] * z2 + cw_ref[1:2, :] * z1 + cw_ref[2:3, :] * z
    a_ref[0] = (b_gate * y).astype(BF16)
    tail = z[tm - SUBLANES:tm, :]
    carry_ref[...] = tail
    st_ref[0] = tail


def conv_prompt(x3, shift, scale, w_in, conv_w8, tm=FFN_ROW_TILE):
    b, t, d = x3.shape
    mod_spec = pl.BlockSpec((1, 1, d), lambda bi, i: (bi, 0, 0))
    return pl.pallas_call(
        _conv_prompt_kernel,
        grid=(b, t // tm),
        in_specs=[pl.BlockSpec((1, tm, d), lambda bi, i: (bi, i, 0)), mod_spec, mod_spec,
                  pl.BlockSpec(w_in.shape, lambda bi, i: (0, 0), pipeline_mode=pl.Buffered(1)),
                  pl.BlockSpec(conv_w8.shape, lambda bi, i: (0, 0))],
        out_specs=[pl.BlockSpec((1, tm, d), lambda bi, i: (bi, i, 0)),
                   pl.BlockSpec((1, SUBLANES, d), lambda bi, i: (bi, 0, 0))],
        out_shape=[jax.ShapeDtypeStruct((b, t, d), BF16),
                   jax.ShapeDtypeStruct((b, SUBLANES, d), F32)],
        scratch_shapes=[pltpu.VMEM((SUBLANES, d), F32)],
        compiler_params=_params("arbitrary", "arbitrary"),
        name="conv_prompt",
    )(x3, shift, scale, w_in, conv_w8)


def _conv_sample_kernel(x_ref, sh_ref, sc_ref, win_ref, cw_ref, p0_ref, p1_ref, a_ref, z_ref):
    d = x_ref.shape[-1]
    hb = _modulate(x_ref[...], sh_ref[0], sc_ref[0]).astype(BF16)
    b_gate = _dot(hb, win_ref[:, 0:d])
    z = _dot(hb, win_ref[:, d:2 * d]) * _dot(hb, win_ref[:, 2 * d:3 * d])
    y = cw_ref[0:1, :] * p0_ref[...] + cw_ref[1:2, :] * p1_ref[...] + cw_ref[2:3, :] * z
    a_ref[...] = (b_gate * y).astype(BF16)
    z_ref[...] = z


def conv_sample(x, shift, scale, w_in, conv_w8, prefix0, prefix1):
    n, d = x.shape
    full = pl.BlockSpec((n, d), lambda i: (0, 0))
    mod_spec = pl.BlockSpec((1, n, d), lambda i: (0, 0, 0))
    return pl.pallas_call(
        _conv_sample_kernel,
        grid=(1,),
        in_specs=[full, mod_spec, mod_spec,
                  pl.BlockSpec(w_in.shape, lambda i: (0, 0), pipeline_mode=pl.Buffered(1)),
                  pl.BlockSpec(conv_w8.shape, lambda i: (0, 0)), full, full],
        out_specs=[full, full],
        out_shape=[jax.ShapeDtypeStruct((n, d), BF16), jax.ShapeDtypeStruct((n, d), F32)],
        compiler_params=_params("arbitrary"),
        name="conv_sample",
    )(x, shift, scale, w_in, conv_w8, prefix0, prefix1)


def kernel(x_prompt, x_sample, cache_k, cache_v, cache_kidx, state_conv, page_table, c_prompt, c_sample, w_ada, b_ada, w_ffn_in, w_ffn_out, w_attn_in, w_attn_out, q_norm_gain, k_norm_gain, w_conv_in, conv_w, w_conv_out):
    b, t, d = x_prompt.shape
    db, dt, _ = x_sample.shape
    assert dt == 1, "the sample path handles one new token per sequence"
    depth = w_ada.shape[0]
    head_dim = q_norm_gain.shape[-1]
    n_heads = w_attn_out.shape[1] // head_dim
    kvh = cache_k.shape[3]
    idx_dim = cache_kidx.shape[-1]
    hd, kvd = n_heads * head_dim, kvh * head_dim
    idx_heads = (w_attn_in.shape[-1] - hd - 2 * kvd - idx_dim) // (idx_dim + 1)
    ihd = idx_heads * idx_dim
    conv_width = conv_w.shape[1]
    assert conv_width == 3
    dims = dict(n_heads=n_heads, group=n_heads // kvh, head_dim=head_dim, hd=hd, kvd=kvd, ihd=ihd,
                idx_dim=idx_dim, idx_heads=idx_heads, v_rows=head_dim + BF16_SUBLANES,
                qk_scale=head_dim ** -0.5,
                wi_scale=idx_heads ** -0.5 * idx_dim ** -0.5)

    n_c = b + db
    n_c_pad = -(-n_c // SUBLANES) * SUBLANES
    c_all = jnp.concatenate([c_prompt, c_sample, jnp.zeros((n_c_pad - n_c, d), F32)], axis=0)
    mods = ada_mods(c_all, w_ada, b_ada).reshape(depth, n_c_pad, N_MOD, d)

    def prompt_mod(layer, j):
        return mods[layer, :b, j][:, None, :]

    def sample_mod(layer, j):
        return mods[layer, b:n_c, j][None]

    gw = 256 if kvd % 256 == 0 and hd % 256 == 0 else kvd
    gi = jnp.arange(gw) // head_dim
    g_mat = jnp.where(gi[:, None] == gi[None, :], 1.0 / head_dim, 0.0).astype(BF16)

    n_attn, n_pool, page = cache_k.shape[:3]
    cache_kt = jnp.transpose(cache_k, (0, 1, 3, 4, 2)).reshape(n_attn, n_pool, kvd, page)
    cache_vt = jnp.transpose(cache_v, (0, 1, 3, 4, 2)).reshape(n_attn, n_pool, kvd, page)
    cache_kit = jnp.transpose(cache_kidx, (0, 1, 3, 2))

    w_in_bf, w_out_bf = w_ffn_in.astype(BF16), w_ffn_out.astype(BF16)
    xp = x_prompt.reshape(b * t, d)
    xs = x_sample.reshape(db, d)
    pk, pv, pki, pconv, sk, sv, ski, sconv = [], [], [], [], [], [], [], []
    n_mixers = 2
    for layer in range(depth):
        li = layer // n_mixers
        pm = [prompt_mod(layer, j) for j in range(N_MOD)]
        sm = [sample_mod(layer, j) for j in range(N_MOD)]
        xp = ffn_call(xp, pm[0], pm[1], pm[2], w_in_bf, w_out_bf, (layer, 0))
        xs = ffn_call(xs, sm[0], sm[1], sm[2], w_in_bf, w_out_bf, (layer, 0))
        if layer % n_mixers == 0:
            wa = w_attn_in[li].astype(BF16)
            o1, o2, o3, o4, o5 = hd, hd + kvd, hd + 2 * kvd, hd + 2 * kvd + ihd, hd + 2 * kvd + ihd + idx_dim
            wkw = jnp.pad(wa[:, o4:], ((0, 0), (0, 128 - (idx_dim + idx_heads))))
            aw = dict(wq=wa[:, :o1], wk=wa[:, o1:o2], wv=wa[:, o2:o3], wqi=wa[:, o3:o4], wkw=wkw,
                      gq=jnp.tile(q_norm_gain[li], n_heads)[None, :],
                      gk=jnp.tile(k_norm_gain[li], kvh)[None, :], g=g_mat)
            w_mix_out = w_attn_out[li].astype(BF16)
            qt, k, kb, v, vt, qit, ki, kib, wit = attn_proj_prompt(xp.reshape(b, t, d), pm[3], pm[4], aw, dims)
            ap = dsa_prompt(qt, qit, wit, kb, vt, kib, dims).reshape(b * t, hd)
            pk.append(k.reshape(b, t, kvh, head_dim))
            pv.append(v.reshape(b, t, kvh, head_dim))
            pki.append(ki)
            qs, ks, vs, qis, kis, wis = attn_proj_sample(xs, sm[3], sm[4], aw, dims)
            a_s = dsa_sample(qs, ks, vs, qis, kis, wis, cache_kt, cache_vt, cache_kit, li,
                             page_table, dims)
            sk.append(ks.reshape(db, 1, kvh, head_dim))
            sv.append(vs.reshape(db, 1, kvh, head_dim))
            ski.append(kis.reshape(db, 1, idx_dim))
        else:
            wc_in = w_conv_in[li].astype(BF16)
            w_mix_out = w_conv_out[li].astype(BF16)
            cw8 = jnp.pad(conv_w[li], ((0, SUBLANES - conv_width), (0, 0)))
            ap, st = conv_prompt(xp.reshape(b, t, d), pm[3], pm[4], wc_in, cw8)
            ap = ap.reshape(b * t, d)
            pconv.append(st[:, SUBLANES - (conv_width - 1):, :])
            prefix = state_conv[li]
            a_s, zs = conv_sample(xs, sm[3], sm[4], wc_in, cw8, prefix[:, 0], prefix[:, 1])
            sconv.append(jnp.stack([prefix[:, 1], zs], axis=1))
        xp = ffn_call(xp, pm[6], pm[7], pm[8], w_in_bf, w_out_bf, (layer, 1), pre=(ap, w_mix_out, pm[5]))
        xs = ffn_call(xs, sm[6], sm[7], sm[8], w_in_bf, w_out_bf, (layer, 1), pre=(a_s, w_mix_out, sm[5]))

    return (xp.reshape(b, t, d), xs.reshape(db, 1, d),
            jnp.stack(pk), jnp.stack(pv), jnp.stack(pki), jnp.stack(pconv),
            jnp.stack(sk), jnp.stack(sv), jnp.stack(ski), jnp.stack(sconv))
```
